```python
import math
import jax, jax.numpy as jnp
from jax import lax
import numpy as np

D_MODEL = 1024
BATCH = 1
SEQ = 16384
DEPTH = 2

HEAD_DIM = 64
GRID_W = 64
RMS_EPS = 1e-6
NEG = -1e30

A_HEADS = 8
B_HEADS = 8
DILATED_BRANCHES = ((128, 1), (512, 4), (2048, 16))
WIN_BLOCK = 128
NA_ROWS = 8
NA_COLS = 16

C_HEADS = 16
Q_LORA = 384
KV_LORA = 128
QK_NOPE = 64
QK_ROPE = 32
V_DIM = 64
ROPE_THETA = 10000.0
Q_BLOCK = 128

T5_BUCKETS = 32
T5_MAX_DIST = 1024
T5_HEADS = max(A_HEADS, C_HEADS)

D_FF = -(-8 * D_MODEL // (3 * 256)) * 256

N_EVEN = (DEPTH + 1) // 2
N_ODD = DEPTH // 2

kernel_name = "hybrid_dilated_natten_mla_encoder"


def rmsnorm(x, g):
    xf = x.astype(jnp.float32)
    y = xf * lax.rsqrt(jnp.mean(xf * xf, axis=-1, keepdims=True) + RMS_EPS)
    return (y * g.astype(jnp.float32)).astype(x.dtype)


def t5_bucket(rel):
    half = T5_BUCKETS // 2
    max_exact = half // 2
    n = jnp.abs(rel)
    nf = jnp.maximum(n, max_exact).astype(jnp.float32)
    large = max_exact + (jnp.log(nf / max_exact) / math.log(T5_MAX_DIST / max_exact)
                         * (half - max_exact)).astype(jnp.int32)
    large = jnp.minimum(large, half - 1)
    return jnp.where(rel > 0, half, 0) + jnp.where(n < max_exact, n, large)


def t5_bias(t5_table, rel, n_heads):
    return t5_table[:, :n_heads].T[:, t5_bucket(rel)].astype(jnp.float32)


def dilated_branch(q, k, v, t5_table, window, dil):
    B, S, H, D = q.shape
    side = window // (2 * dil)
    L = S // dil
    nb = -(-L // WIN_BLOCK)
    Lp = nb * WIN_BLOCK
    span = WIN_BLOCK + 2 * side

    def to_sub(a):
        return a.reshape(B, L, dil, H, D).transpose(0, 2, 1, 3, 4)

    qs = jnp.pad(to_sub(q), ((0, 0), (0, 0), (0, Lp - L), (0, 0), (0, 0)))
    qs = qs.reshape(B, dil, nb, WIN_BLOCK, H, D)
    kv_pad = ((0, 0), (0, 0), (side, Lp - L + side), (0, 0), (0, 0))
    ks = jnp.pad(to_sub(k), kv_pad)
    vs = jnp.pad(to_sub(v), kv_pad)
    idx = jnp.arange(nb)[:, None] * WIN_BLOCK + jnp.arange(span)[None, :]
    kb = ks[:, :, idx]
    vb = vs[:, :, idx]

    s = jnp.einsum('brnqhd,brnkhd->brnhqk', qs, kb).astype(jnp.float32) * (D ** -0.5)
    j = jnp.arange(span)[None, :] - side - jnp.arange(WIN_BLOCK)[:, None]
    mk = idx - side
    valid = (jnp.abs(j) <= side)[None] & ((mk >= 0) & (mk < L))[:, None, :]
    s = s + t5_bias(t5_table, j * dil, H)[None, None, None]
    s = jnp.where(valid[None, None, :, None], s, NEG)
    m = jnp.max(s, axis=-1, keepdims=True)
    p = jnp.exp(s - m)
    den = jnp.sum(p, axis=-1, keepdims=True)
    o = jnp.einsum('brnhqk,brnkhd->brnqhd', (p / den).astype(v.dtype), vb)
    lse = (m + jnp.log(den))[..., 0]

    o = o.reshape(B, dil, Lp, H, D)[:, :, :L].transpose(0, 2, 1, 3, 4).reshape(B, S, H, D)
    lse = lse.transpose(0, 1, 2, 4, 3).reshape(B, dil, Lp, H)[:, :, :L]
    lse = lse.transpose(0, 2, 1, 3).reshape(B, S, H)
    return o, lse


def dilated_mixture(q, k, v, t5_table):
    outs, lses = [], []
    for window, dil in DILATED_BRANCHES:
        o, l = dilated_branch(q, k, v, t5_table, window, dil)
        outs.append(o)
        lses.append(l)
    w = jax.nn.softmax(jnp.stack(lses, axis=0), axis=0)
    o = jnp.sum(w[..., None] * jnp.stack(outs, axis=0).astype(jnp.float32), axis=0)
    return o.astype(q.dtype)


def neighbourhood_attention(q, k, v, rpb):
    B, S, H, D = q.shape
    rows = S // GRID_W
    kr = min(NA_ROWS, rows)
    kc = NA_COLS
    qg = q.reshape(B, rows, GRID_W, H, D)
    kg = k.reshape(B, rows, GRID_W, H, D)
    vg = v.reshape(B, rows, GRID_W, H, D)
    r = jnp.arange(rows)
    r0 = jnp.clip(r - kr // 2, 0, rows - kr)
    row_idx = r0[:, None] + jnp.arange(kr)[None, :]
    kb = kg[:, row_idx]
    vb = vg[:, row_idx]
    c = jnp.arange(GRID_W)
    c0 = jnp.clip(c - kc // 2, 0, GRID_W - kc)
    col_ok = (c[None, :] >= c0[:, None]) & (c[None, :] < c0[:, None] + kc)

    s = jnp.einsum('brqhd,brkwhd->bhrqkw', qg, kb).astype(jnp.float32) * (D ** -0.5)
    roff = row_idx - r[:, None] + (NA_ROWS - 1)
    coff = jnp.clip(c[None, :] - c[:, None] + (NA_COLS - 1), 0, 2 * NA_COLS - 2)
    bias = rpb[:, roff[:, None, :, None], coff[None, :, None, :]]
    s = s + bias[None].astype(jnp.float32)
    s = jnp.where(col_ok[:, None, :], s, NEG)
    sh = s.shape
    p = jax.nn.softmax(s.reshape(sh[:4] + (kr * GRID_W,)), axis=-1).reshape(sh)
    o = jnp.einsum('bhrqkw,brkwhd->brqhd', p.astype(v.dtype), vb)
    return o.reshape(B, S, H, D)


def even_mixer(h, w_in, rpb, w_out, t5_table):
    B, S, _ = h.shape
    proj = h @ w_in
    a_part = proj[..., :3 * A_HEADS * HEAD_DIM]
    b_part = proj[..., 3 * A_HEADS * HEAD_DIM:]
    qa, ka, va = [t.reshape(B, S, A_HEADS, HEAD_DIM) for t in jnp.split(a_part, 3, axis=-1)]
    qb, kb, vb = [t.reshape(B, S, B_HEADS, HEAD_DIM) for t in jnp.split(b_part, 3, axis=-1)]
    oa = dilated_mixture(qa, ka, va, t5_table)
    ob = neighbourhood_attention(qb, kb, vb, rpb)
    o = jnp.concatenate([oa.reshape(B, S, -1), ob.reshape(B, S, -1)], axis=-1)
    return o @ w_out


def rotate(x, cos, sin):
    half = x.shape[-1] // 2
    x1, x2 = x[..., :half], x[..., half:]
    return jnp.concatenate([x1 * cos - x2 * sin, x1 * sin + x2 * cos], axis=-1)


def mla_mixer(h, w_in, q_norm, w_uq, kv_norm, w_uk, w_uv, w_out, t5_table):
    B, S, _ = h.shape
    H = C_HEADS
    proj = h @ w_in
    cq = rmsnorm(proj[..., :Q_LORA], q_norm)
    ckv = rmsnorm(proj[..., Q_LORA:Q_LORA + KV_LORA], kv_norm)
    kr = proj[..., Q_LORA + KV_LORA:]

    pos = jnp.arange(S, dtype=jnp.float32)
    freqs = ROPE_THETA ** (-jnp.arange(0, QK_ROPE, 2, dtype=jnp.float32) / QK_ROPE)
    ang = pos[:, None] * freqs[None, :]
    cos = jnp.cos(ang).astype(h.dtype)
    sin = jnp.sin(ang).astype(h.dtype)

    q = (cq @ w_uq).reshape(B, S, H, QK_NOPE + QK_ROPE)
    q_nope = q[..., :QK_NOPE]
    q_rope = rotate(q[..., QK_NOPE:], cos[None, :, None], sin[None, :, None])
    k_rope = rotate(kr, cos[None], sin[None])
    k_nope = (ckv @ w_uk).reshape(B, S, H, QK_NOPE)
    v = (ckv @ w_uv).reshape(B, S, H, V_DIM)
    scale = (QK_NOPE + QK_ROPE) ** -0.5

    nb = S // Q_BLOCK
    qn_blocks = q_nope.reshape(B, nb, Q_BLOCK, H, QK_NOPE).transpose(1, 0, 2, 3, 4)
    qr_blocks = q_rope.reshape(B, nb, Q_BLOCK, H, QK_ROPE).transpose(1, 0, 2, 3, 4)
    kpos = jnp.arange(S)

    def block(args):
        qn, qr, i = args
        s = (jnp.einsum('bqhd,bkhd->bhqk', qn, k_nope)
             + jnp.einsum('bqhd,bkd->bhqk', qr, k_rope)).astype(jnp.float32) * scale
        qpos = i * Q_BLOCK + jnp.arange(Q_BLOCK)
        s = s + t5_bias(t5_table, kpos[None, :] - qpos[:, None], H)[None]
        p = jax.nn.softmax(s, axis=-1)
        return jnp.einsum('bhqk,bkhd->bqhd', p.astype(v.dtype), v)

    o = lax.map(block, (qn_blocks, qr_blocks, jnp.arange(nb)))
    o = o.transpose(1, 0, 2, 3, 4).reshape(B, S, H * V_DIM)
    return o @ w_out


def swiglu(h, w_gate, w_up, w_down):
    return (jax.nn.silu(h @ w_gate) * (h @ w_up)) @ w_down


def setup_inputs(seed: int = 0) -> dict:
    key = jax.random.key(seed)
    ks = jax.random.split(key, 20)
    f32 = jnp.float32

    def w(k, shape, fan_in):
        return jax.random.normal(k, shape, f32) * (fan_in ** -0.5)

    def gain(k, shape):
        return 1.0 + 0.01 * jax.random.normal(k, shape, f32)

    mix_a_b = (A_HEADS + B_HEADS) * HEAD_DIM
    return {
        "x": jax.random.normal(ks[0], (BATCH, SEQ, D_MODEL), f32),
        "t5_table": 0.1 * jax.random.normal(ks[1], (T5_BUCKETS, T5_HEADS), f32),
        "attn_norm": gain(ks[2], (DEPTH, D_MODEL)),
        "ffn_norm": gain(ks[3], (DEPTH, D_MODEL)),
        "even_w_in": w(ks[4], (N_EVEN, D_MODEL, 3 * mix_a_b), D_MODEL),
        "na_rpb": 0.1 * jax.random.normal(ks[5], (N_EVEN, B_HEADS, 2 * NA_ROWS - 1, 2 * NA_COLS - 1), f32),
        "even_w_out": w(ks[6], (N_EVEN, mix_a_b, D_MODEL), mix_a_b),
        "odd_w_in": w(ks[7], (N_ODD, D_MODEL, Q_LORA + KV_LORA + QK_ROPE), D_MODEL),
        "mla_q_norm": gain(ks[8], (N_ODD, Q_LORA)),
        "mla_w_uq": w(ks[9], (N_ODD, Q_LORA, C_HEADS * (QK_NOPE + QK_ROPE)), Q_LORA),
        "mla_kv_norm": gain(ks[10], (N_ODD, KV_LORA)),
        "mla_w_uk": w(ks[11], (N_ODD, KV_LORA, C_HEADS * QK_NOPE), KV_LORA),
        "mla_w_uv": w(ks[12], (N_ODD, KV_LORA, C_HEADS * V_DIM), KV_LORA),
        "odd_w_out": w(ks[13], (N_ODD, C_HEADS * V_DIM, D_MODEL), C_HEADS * V_DIM),
        "ffn_w_gate": w(ks[14], (DEPTH, D_MODEL, D_FF), D_MODEL),
        "ffn_w_up": w(ks[15], (DEPTH, D_MODEL, D_FF), D_MODEL),
        "ffn_w_down": w(ks[16], (DEPTH, D_FF, D_MODEL), D_FF),
        "final_norm": gain(ks[17], (D_MODEL,)),
    }


def reference(x, t5_table, attn_norm, ffn_norm, even_w_in, na_rpb, even_w_out,
              odd_w_in, mla_q_norm, mla_w_uq, mla_kv_norm, mla_w_uk, mla_w_uv, odd_w_out,
              ffn_w_gate, ffn_w_up, ffn_w_down, final_norm):
    for layer in range(DEPTH):
        h = rmsnorm(x, attn_norm[layer])
        if layer % 2 == 0:
            i = layer // 2
            x = x + even_mixer(h, even_w_in[i], na_rpb[i], even_w_out[i], t5_table)
        else:
            i = layer // 2
            x = x + mla_mixer(h, odd_w_in[i], mla_q_norm[i], mla_w_uq[i], mla_kv_norm[i],
                              mla_w_uk[i], mla_w_uv[i], odd_w_out[i], t5_table)
        h = rmsnorm(x, ffn_norm[layer])
        x = x + swiglu(h, ffn_w_gate[layer], ffn_w_up[layer], ffn_w_down[layer])
    return rmsnorm(x, final_norm)
```

```python
import functools
import math

import numpy as np
import jax
import jax.numpy as jnp
from jax import lax
from jax.experimental import pallas as pl
from jax.experimental.pallas import tpu as pltpu

F32 = jnp.float32
BF16 = jnp.bfloat16

HEAD_DIM = 64
GRID_W = 64
RMS_EPS = 1e-6
A_HEADS = 8
B_HEADS = 8
DILATED_BRANCHES = ((128, 1), (512, 4), (2048, 16))
NA_ROWS = 8
NA_COLS = 16
C_HEADS = 16
Q_LORA = 384
KV_LORA = 128
QK_NOPE = 64
QK_ROPE = 32
V_DIM = 64
ROPE_THETA = 10000.0
T5_BUCKETS = 32
T5_MAX_DIST = 1024

LOG2E = math.log2(math.e)
NEG = -1e30
LANES = 128
VMEM_LIMIT = 56 * 1024 * 1024


def _t5_upper_bounds():
    half = T5_BUCKETS // 2
    max_exact = half // 2
    n = np.arange(0, 4 * T5_MAX_DIST, dtype=np.int64)
    nf = np.maximum(n, max_exact).astype(np.float32)
    val = (np.log(nf / np.float32(max_exact)) / np.float32(math.log(T5_MAX_DIST / max_exact))
           * np.float32(half - max_exact))
    large = np.minimum(max_exact + val.astype(np.int32), half - 1)
    bucket = np.where(n < max_exact, n, large)
    return tuple(int(np.argmax(bucket > b)) for b in range(half - 1))


T5_UPPER = _t5_upper_bounds()
T5_FAR = T5_UPPER[-1]


def _t5_bias_tile(t5_ref, head, rel):
    half = T5_BUCKETS // 2
    n = jnp.abs(rel)
    vneg = jnp.full(rel.shape, t5_ref[half - 1, head], F32)
    vpos = jnp.full(rel.shape, t5_ref[2 * half - 1, head], F32)
    for b in range(half - 2, -1, -1):
        inb = n < T5_UPPER[b]
        vneg = jnp.where(inb, t5_ref[b, head], vneg)
        vpos = jnp.where(inb, t5_ref[half + b, head], vpos)
    return jnp.where(rel > 0, vpos, vneg) * LOG2E


def _rms(x, g):
    return x * lax.rsqrt(jnp.mean(x * x, axis=-1, keepdims=True) + RMS_EPS) * g


def _nt_dot(a, b):
    return lax.dot_general(a, b, (((1,), (1,)), ((), ())), preferred_element_type=F32)


def _resident(shape, index_map):
    return pl.BlockSpec(shape, index_map, pipeline_mode=pl.Buffered(1))


def _norm_proj_kernel(x_ref, g_ref, w_ref, o_ref):
    h = _rms(x_ref[...], g_ref[...]).astype(BF16)
    o_ref[...] = jnp.dot(h, w_ref[...], preferred_element_type=F32).astype(o_ref.dtype)


def _norm_proj(x, g, w, tm):
    s, d = x.shape
    n = w.shape[1]
    return pl.pallas_call(
        _norm_proj_kernel,
        grid=(s // tm,),
        in_specs=[
            pl.BlockSpec((tm, d), lambda i: (i, 0)),
            _resident((1, d), lambda i: (0, 0)),
            _resident((d, n), lambda i: (0, 0)),
        ],
        out_specs=pl.BlockSpec((tm, n), lambda i: (i, 0)),
        out_shape=jax.ShapeDtypeStruct((s, n), BF16),
        compiler_params=pltpu.CompilerParams(
            dimension_semantics=("parallel",), vmem_limit_bytes=VMEM_LIMIT),
        name="norm_proj",
    )(x, g.reshape(1, d), w)


DIL_SIDE = 64
DIL_REACH = max(w // 2 for w, _ in DILATED_BRANCHES)
DIL_TQ = 256
DIL_W = DIL_TQ + 2 * DIL_REACH


def _dilated_kernel(t5_ref, q_ref, k_ref, v_ref, o_ref, bias_ref, *, seq):
    hp = pl.program_id(0)
    i = pl.program_id(1)

    @pl.when(i == 0)
    def _():
        row = lax.broadcasted_iota(jnp.int32, (DIL_TQ, DIL_W), 0)
        col = lax.broadcasted_iota(jnp.int32, (DIL_TQ, DIL_W), 1)
        rel = col - row - DIL_REACH
        n = jnp.abs(rel)
        count = jnp.zeros(rel.shape, jnp.int32)
        for window, dil in DILATED_BRANCHES:
            member = ((rel & (dil - 1)) == 0) & (n <= (window // 2))
            count = count + member.astype(jnp.int32)
        logmult = jnp.where(count == 3, math.log2(3.0), jnp.where(count == 2, 1.0, 0.0))
        for hh in range(2):
            b = _t5_bias_tile(t5_ref, 2 * hp + hh, rel) + logmult
            bias_ref[hh] = jnp.where(count > 0, b, NEG)

    start = pl.multiple_of(i * DIL_TQ, DIL_TQ)
    kw = k_ref[pl.ds(start, DIL_W), :]
    vw = v_ref[pl.ds(start, DIL_W), :]
    q = q_ref[...]
    lane = lax.broadcasted_iota(jnp.int32, q.shape, 1)
    kpos = start + lax.broadcasted_iota(jnp.int32, (1, DIL_W), 1) - DIL_REACH
    edge = jnp.where((kpos >= 0) & (kpos < seq), 0.0, NEG)

    outs = []
    for hh in range(2):
        in_head = (lane >= hh * HEAD_DIM) & (lane < (hh + 1) * HEAD_DIM)
        qm = jnp.where(in_head, q, jnp.zeros_like(q))
        s = _nt_dot(qm, kw) + bias_ref[hh] + edge
        m = jnp.max(s, axis=-1, keepdims=True)
        p = jnp.exp2(s - m)
        l = jnp.sum(p, axis=-1, keepdims=True)
        o = jnp.dot(p.astype(BF16), vw, preferred_element_type=F32)
        outs.append(o / l)
    o_ref[...] = jnp.where(lane < HEAD_DIM, outs[0], outs[1]).astype(o_ref.dtype)


def _dilated(proj, kpad, vpad, t5_table):
    s = proj.shape[0]
    sp = kpad.shape[0]
    pairs = A_HEADS // 2
    return pl.pallas_call(
        functools.partial(_dilated_kernel, seq=s),
        grid=(pairs, s // DIL_TQ),
        in_specs=[
            pl.BlockSpec(memory_space=pltpu.SMEM),
            pl.BlockSpec((DIL_TQ, LANES), lambda h, i: (i, h)),
            _resident((sp, LANES), lambda h, i: (0, h)),
            _resident((sp, LANES), lambda h, i: (0, h)),
        ],
        out_specs=pl.BlockSpec((DIL_TQ, LANES), lambda h, i: (i, h)),
        out_shape=jax.ShapeDtypeStruct((s, A_HEADS * HEAD_DIM), BF16),
        scratch_shapes=[pltpu.VMEM((2, DIL_TQ, DIL_W), F32)],
        compiler_params=pltpu.CompilerParams(
            dimension_semantics=("arbitrary", "arbitrary"), vmem_limit_bytes=VMEM_LIMIT),
        name="dilated",
    )(t5_table, proj, kpad, vpad)


NA_QROWS = 8
NA_KROWS = 16
NA_TQ = NA_QROWS * GRID_W
NA_TK = NA_KROWS * GRID_W
NA_RO = 2 * NA_ROWS - 1
NA_CO = 2 * NA_COLS - 1


def _natten_kernel(rpb_ref, q_ref, k_ref, v_ref, o_ref, tile_ref, bias_ref, *, rows):
    hp = pl.program_id(0)
    qi = pl.program_id(1)
    nq = pl.num_programs(1)
    rb = qi * NA_QROWS
    kb0 = jnp.clip(rb - NA_ROWS // 2, 0, rows - NA_KROWS)

    @pl.when(qi == 0)
    def _():
        c = lax.broadcasted_iota(jnp.int32, (GRID_W, LANES), 0)
        kc = lax.broadcasted_iota(jnp.int32, (GRID_W, LANES), 1) & (GRID_W - 1)
        co = jnp.clip(kc - c + (NA_COLS - 1), 0, NA_CO - 1)
        c0 = jnp.clip(c - NA_COLS // 2, 0, GRID_W - NA_COLS)
        col_ok = (kc >= c0) & (kc < c0 + NA_COLS)
        for hh in range(2):
            head = 2 * hp + hh

            def body(ro, carry):
                base = (head * NA_RO + ro) * NA_CO
                v = jnp.full(co.shape, rpb_ref[base + NA_CO - 1], F32)
                for k in range(NA_CO - 2, -1, -1):
                    v = jnp.where(co == k, rpb_ref[base + k], v)
                tile_ref[hh, ro] = jnp.where(col_ok, v * LOG2E, NEG)
                return carry

            lax.fori_loop(0, NA_RO, body, 0)

    @pl.when((qi == 0) | (qi == 1) | (qi == nq - 1))
    def _():
        lane = lax.broadcasted_iota(jnp.int32, (GRID_W, LANES), 1)
        left = lane < GRID_W
        for i in range(NA_QROWS):
            r = rb + i
            r0 = jnp.clip(r - NA_ROWS // 2, 0, rows - NA_ROWS)
            for a in range(NA_KROWS // 2):
                kr = kb0 + 2 * a
                ro = kr - r + (NA_ROWS - 1)
                ok0 = (kr >= r0) & (kr < r0 + NA_ROWS)
                ok1 = (kr + 1 >= r0) & (kr + 1 < r0 + NA_ROWS)
                ro0 = jnp.clip(ro, 0, NA_RO - 1)
                ro1 = jnp.clip(ro + 1, 0, NA_RO - 1)
                for hh in range(2):
                    t0 = jnp.where(ok0, tile_ref[hh, ro0], NEG)
                    t1 = jnp.where(ok1, tile_ref[hh, ro1], NEG)
                    bias_ref[hh, i * GRID_W:(i + 1) * GRID_W, a * LANES:(a + 1) * LANES] = (
                        jnp.where(left, t0, t1))

    kstart = pl.multiple_of(kb0 * GRID_W, GRID_W * 4)
    kw = k_ref[pl.ds(kstart, NA_TK), :]
    vw = v_ref[pl.ds(kstart, NA_TK), :]
    q = q_ref[...]
    lane = lax.broadcasted_iota(jnp.int32, q.shape, 1)
    outs = []
    for hh in range(2):
        in_head = (lane >= hh * HEAD_DIM) & (lane < (hh + 1) * HEAD_DIM)
        qm = jnp.where(in_head, q, jnp.zeros_like(q))
        s = _nt_dot(qm, kw) + bias_ref[hh]
        m = jnp.max(s, axis=-1, keepdims=True)
        p = jnp.exp2(s - m)
        l = jnp.sum(p, axis=-1, keepdims=True)
        o = jnp.dot(p.astype(BF16), vw, preferred_element_type=F32)
        outs.append(o / l)
    o_ref[...] = jnp.where(lane < HEAD_DIM, outs[0], outs[1]).astype(o_ref.dtype)


def _natten(proj, rpb_flat, col0):
    s = proj.shape[0]
    rows = s // GRID_W
    pairs = B_HEADS // 2
    return pl.pallas_call(
        functools.partial(_natten_kernel, rows=rows),
        grid=(pairs, s // NA_TQ),
        in_specs=[
            pl.BlockSpec(memory_space=pltpu.SMEM),
            pl.BlockSpec((NA_TQ, LANES), lambda h, i: (i, col0 + h)),
            _resident((s, LANES), lambda h, i: (0, col0 + pairs + h)),
            _resident((s, LANES), lambda h, i: (0, col0 + 2 * pairs + h)),
        ],
        out_specs=pl.BlockSpec((NA_TQ, LANES), lambda h, i: (i, h)),
        out_shape=jax.ShapeDtypeStruct((s, B_HEADS * HEAD_DIM), BF16),
        scratch_shapes=[
            pltpu.VMEM((2, NA_RO, GRID_W, LANES), F32),
            pltpu.VMEM((2, NA_TQ, NA_TK), F32),
        ],
        compiler_params=pltpu.CompilerParams(
            dimension_semantics=("arbitrary", "arbitrary"), vmem_limit_bytes=VMEM_LIMIT),
        name="natten",
    )(rpb_flat, proj, proj, proj)


FF_CHUNK = 256


def _mix_ffn_kernel(*refs, n_o, final):
    x_ref = refs[0]
    o_refs = refs[1:1 + n_o]
    wo_refs = refs[1 + n_o:1 + 2 * n_o]
    g_ref, wg_ref, wu_ref, wd_ref = refs[1 + 2 * n_o:5 + 2 * n_o]
    rest = refs[5 + 2 * n_o:]
    gf_ref = rest[0] if final else None
    out_ref = rest[-1]

    y = x_ref[...]
    for o_ref, wo_ref in zip(o_refs, wo_refs):
        y = y + jnp.dot(o_ref[...], wo_ref[...], preferred_element_type=F32)
    h = _rms(y, g_ref[...]).astype(BF16)
    d_ff = wg_ref.shape[1]
    acc = jnp.zeros(y.shape, F32)
    for c in range(0, d_ff, FF_CHUNK):
        gate = jnp.dot(h, wg_ref[:, c:c + FF_CHUNK], preferred_element_type=F32)
        up = jnp.dot(h, wu_ref[:, c:c + FF_CHUNK], preferred_element_type=F32)
        act = (gate * jax.nn.sigmoid(gate) * up).astype(BF16)
        acc = acc + jnp.dot(act, wd_ref[c:c + FF_CHUNK, :], preferred_element_type=F32)
    y = y + acc
    if final:
        y = _rms(y, gf_ref[...])
    out_ref[...] = y


def _mix_ffn(x, o_parts, wo_parts, g, wg, wu, wd, gf, tm):
    s, d = x.shape
    d_ff = wg.shape[1]
    assert d_ff % FF_CHUNK == 0
    n_o = len(o_parts)
    final = gf is not None
    in_specs = [pl.BlockSpec((tm, d), lambda i: (i, 0))]
    in_specs += [pl.BlockSpec((tm, o.shape[1]), lambda i: (i, 0)) for o in o_parts]
    in_specs += [_resident(w.shape, lambda i: (0, 0)) for w in wo_parts]
    in_specs += [
        _resident((1, d), lambda i: (0, 0)),
        _resident((d, d_ff), lambda i: (0, 0)),
        _resident((d, d_ff), lambda i: (0, 0)),
        _resident((d_ff, d), lambda i: (0, 0)),
    ]
    args = [x, *o_parts, *wo_parts, g.reshape(1, d), wg, wu, wd]
    if final:
        in_specs.append(_resident((1, d), lambda i: (0, 0)))
        args.append(gf.reshape(1, d))
    return pl.pallas_call(
        functools.partial(_mix_ffn_kernel, n_o=n_o, final=final),
        grid=(s // tm,),
        in_specs=in_specs,
        out_specs=pl.BlockSpec((tm, d), lambda i: (i, 0)),
        out_shape=jax.ShapeDtypeStruct((s, d), F32),
        compiler_params=pltpu.CompilerParams(
            dimension_semantics=("parallel",), vmem_limit_bytes=VMEM_LIMIT),
        name="mix_ffn",
    )(*args)


ROPE_LANE0 = QK_NOPE
ROPE_HALF = QK_ROPE // 2


def _rope(x, cos_t, sin_lo, sin_hi):
    return (x * cos_t + pltpu.roll(x, LANES - ROPE_HALF, 1) * sin_lo
            + pltpu.roll(x, ROPE_HALF, 1) * sin_hi)


def _mla_prep_kernel(x_ref, g_ref, win_ref, qn_ref, kvn_ref, wuq_ref, wuk_ref, wuv_ref,
                     cos_ref, slo_ref, shi_ref, q_out, k_out, v_out):
    h = _rms(x_ref[...], g_ref[...]).astype(BF16)
    proj = jnp.dot(h, win_ref[...], preferred_element_type=F32)
    cq = _rms(proj[:, :Q_LORA], qn_ref[...]).astype(BF16)
    ckv = _rms(proj[:, Q_LORA:Q_LORA + KV_LORA], kvn_ref[...]).astype(BF16)
    cos_t, sin_lo, sin_hi = cos_ref[...], slo_ref[...], shi_ref[...]
    kr = _rope(proj[:, Q_LORA + KV_LORA:], cos_t, sin_lo, sin_hi)
    q = jnp.dot(cq, wuq_ref[...], preferred_element_type=F32)
    kn = jnp.dot(ckv, wuk_ref[...], preferred_element_type=F32)
    v = jnp.dot(ckv, wuv_ref[...], preferred_element_type=F32)
    lane = lax.broadcasted_iota(jnp.int32, (1, LANES), 1)
    one_col = jnp.where(lane == V_DIM, 1.0, 0.0)
    for hd in range(C_HEADS):
        sl = slice(hd * LANES, (hd + 1) * LANES)
        q_out[hd] = _rope(q[:, sl], cos_t, sin_lo, sin_hi).astype(BF16)
        k_out[hd] = (kn[:, sl] + kr).astype(BF16)
        v_out[hd] = (v[:, sl] + one_col).astype(BF16)


def _mla_prep(x, g, win, qn, kvn, wuq, wuk, wuv, cos_t, sin_lo, sin_hi, tm):
    s, d = x.shape
    const = lambda i: (0, 0)
    row = lambda i: (i, 0)
    head_out = jax.ShapeDtypeStruct((C_HEADS, s, LANES), BF16)
    head_spec = pl.BlockSpec((C_HEADS, tm, LANES), lambda i: (0, i, 0))
    return pl.pallas_call(
        _mla_prep_kernel,
        grid=(s // tm,),
        in_specs=[
            pl.BlockSpec((tm, d), row),
            _resident((1, d), const),
            _resident(win.shape, const),
            _resident((1, Q_LORA), const),
            _resident((1, KV_LORA), const),
            _resident(wuq.shape, const),
            _resident(wuk.shape, const),
            _resident(wuv.shape, const),
            pl.BlockSpec((tm, LANES), row),
            pl.BlockSpec((tm, LANES), row),
            pl.BlockSpec((tm, LANES), row),
        ],
        out_specs=[head_spec, head_spec, head_spec],
        out_shape=[head_out, head_out, head_out],
        compiler_params=pltpu.CompilerParams(
            dimension_semantics=("parallel",), vmem_limit_bytes=VMEM_LIMIT),
        name="mla_prep",
    )(x, g.reshape(1, d), win, qn.reshape(1, -1), kvn.reshape(1, -1), wuq, wuk, wuv,
      cos_t, sin_lo, sin_hi)


MLA_TQ = 512
MLA_TK = 512
MLA_BAND = (T5_FAR - 1 + MLA_TK - 1) // MLA_TK


def _mla_attn_kernel(t5_ref, q_ref, k_ref, v_ref, o_ref, bias_ref, *, seq):
    hp = pl.program_id(0)
    qi = pl.program_id(1)
    nk = seq // MLA_TK
    half = T5_BUCKETS // 2

    @pl.when(qi == 0)
    def _():
        row = lax.broadcasted_iota(jnp.int32, (MLA_TQ, MLA_TK), 0)
        col = lax.broadcasted_iota(jnp.int32, (MLA_TQ, MLA_TK), 1)
        for hh in range(2):
            for d in range(-MLA_BAND, MLA_BAND + 1):
                bias_ref[hh, d + MLA_BAND] = _t5_bias_tile(
                    t5_ref, 2 * hp + hh, col - row + d * MLA_TK)

    lo = jnp.maximum(qi - MLA_BAND, 0)
    hi = jnp.minimum(qi + MLA_BAND + 1, nk)
    lane = lax.broadcasted_iota(jnp.int32, (MLA_TQ, LANES), 1)

    outs = []
    for hh in range(2):
        head = 2 * hp + hh
        q = q_ref[hh]

        def step(kb, carry, bias_of):
            m, acc = carry
            kstart = pl.multiple_of(kb * MLA_TK, MLA_TK)
            kt = k_ref[hh, pl.ds(kstart, MLA_TK), :]
            vt = v_ref[hh, pl.ds(kstart, MLA_TK), :]
            s = _nt_dot(q, kt) + bias_of(kb)
            m_new = jnp.maximum(m, jnp.max(s, axis=-1, keepdims=True))
            alpha = jnp.exp2(m - m_new)
            p = jnp.exp2(s - m_new)
            acc = alpha * acc + jnp.dot(p.astype(BF16), vt, preferred_element_type=F32)
            return m_new, acc

        far_neg = t5_ref[half - 1, head] * LOG2E
        far_pos = t5_ref[2 * half - 1, head] * LOG2E
        carry = (jnp.full((MLA_TQ, 1), NEG, F32), jnp.zeros((MLA_TQ, LANES), F32))
        carry = lax.fori_loop(0, lo, functools.partial(step, bias_of=lambda kb: far_neg), carry)
        carry = lax.fori_loop(
            lo, hi,
            functools.partial(step, bias_of=lambda kb: bias_ref[hh, kb - qi + MLA_BAND]), carry)
        carry = lax.fori_loop(hi, nk, functools.partial(step, bias_of=lambda kb: far_pos), carry)
        acc = carry[1]
        outs.append(acc / acc[:, V_DIM:V_DIM + 1])
    out = jnp.where(lane < V_DIM, outs[0], pltpu.roll(outs[1], V_DIM, 1))
    o_ref[...] = out.astype(o_ref.dtype)


def _mla_attn(q, k, v, t5_table):
    heads, s, _ = q.shape
    pairs = heads // 2
    assert MLA_TQ == MLA_TK
    return pl.pallas_call(
        functools.partial(_mla_attn_kernel, seq=s),
        grid=(pairs, s // MLA_TQ),
        in_specs=[
            pl.BlockSpec(memory_space=pltpu.SMEM),
            pl.BlockSpec((2, MLA_TQ, LANES), lambda h, i: (h, i, 0)),
            _resident((2, s, LANES), lambda h, i: (h, 0, 0)),
            _resident((2, s, LANES), lambda h, i: (h, 0, 0)),
        ],
        out_specs=pl.BlockSpec((MLA_TQ, LANES), lambda h, i: (i, h)),
        out_shape=jax.ShapeDtypeStruct((s, heads * V_DIM), BF16),
        scratch_shapes=[pltpu.VMEM((2, 2 * MLA_BAND + 1, MLA_TQ, MLA_TK), F32)],
        compiler_params=pltpu.CompilerParams(
            dimension_semantics=("arbitrary", "arbitrary"), vmem_limit_bytes=VMEM_LIMIT),
        name="mla_attn",
    )(t5_table, q, k, v)


def _per_head_lanes(w, heads, width, lane0=0):
    k = w.shape[0]
    w = w.reshape(k, heads, width)
    w = jnp.pad(w, ((0, 0), (0, 0), (lane0, LANES - lane0 - width)))
    return w.reshape(k, heads * LANES)


def _rope_tables(s):
    pos = jnp.arange(s, dtype=F32)
    freqs = ROPE_THETA ** (-jnp.arange(0, QK_ROPE, 2, dtype=F32) / QK_ROPE)
    ang = pos[:, None] * freqs[None, :]
    cos, sin = jnp.cos(ang), jnp.sin(ang)
    zeros = jnp.zeros((s, ROPE_HALF), F32)
    tail = LANES - ROPE_LANE0 - QK_ROPE
    cos_t = jnp.concatenate([jnp.ones((s, ROPE_LANE0), F32), cos, cos, jnp.ones((s, tail), F32)], axis=1)
    sin_lo = jnp.concatenate([jnp.zeros((s, ROPE_LANE0), F32), -sin, zeros, jnp.zeros((s, tail), F32)], axis=1)
    sin_hi = jnp.concatenate([jnp.zeros((s, ROPE_LANE0), F32), zeros, sin, jnp.zeros((s, tail), F32)], axis=1)
    return cos_t, sin_lo, sin_hi


def kernel(x, t5_table, attn_norm, ffn_norm, even_w_in, na_rpb, even_w_out, odd_w_in, mla_q_norm,
           mla_w_uq, mla_kv_norm, mla_w_uk, mla_w_uv, odd_w_out, ffn_w_gate, ffn_w_up, ffn_w_down,
           final_norm):
    b, s, d = x.shape
    assert b == 1 and s % (GRID_W * NA_KROWS) == 0 and s % MLA_TQ == 0
    x0 = x.reshape(s, d)
    tm = 512

    a_cols = A_HEADS * HEAD_DIM
    b_cols = B_HEADS * HEAD_DIM
    qscale = HEAD_DIM ** -0.5 * LOG2E
    w_in = even_w_in[0]
    col_scale = jnp.concatenate([
        jnp.full((a_cols,), qscale, F32), jnp.ones((2 * a_cols,), F32),
        jnp.full((b_cols,), qscale, F32), jnp.ones((2 * b_cols,), F32)])
    proj = _norm_proj(x0, attn_norm[0], (w_in * col_scale).astype(BF16), tm)
    pad = ((DIL_REACH, DIL_REACH), (0, 0))
    kpad = jnp.pad(proj[:, a_cols:2 * a_cols], pad)
    vpad = jnp.pad(proj[:, 2 * a_cols:3 * a_cols], pad)
    oa = _dilated(proj, kpad, vpad, t5_table)
    ob = _natten(proj, na_rpb[0].reshape(-1), 3 * a_cols // LANES)
    w_out = even_w_out[0].astype(BF16)
    x1 = _mix_ffn(x0, [oa, ob], [w_out[:a_cols], w_out[a_cols:]], ffn_norm[0],
                  ffn_w_gate[0].astype(BF16), ffn_w_up[0].astype(BF16), ffn_w_down[0].astype(BF16),
                  None, tm)

    cscale = (QK_NOPE + QK_ROPE) ** -0.5 * LOG2E
    w_in1 = odd_w_in[0]
    win = jnp.concatenate([
        w_in1[:, :Q_LORA + KV_LORA],
        _per_head_lanes(w_in1[:, Q_LORA + KV_LORA:], 1, QK_ROPE, ROPE_LANE0)], axis=1).astype(BF16)
    wuq = (_per_head_lanes(mla_w_uq[0], C_HEADS, QK_NOPE + QK_ROPE) * cscale).astype(BF16)
    wuk = _per_head_lanes(mla_w_uk[0], C_HEADS, QK_NOPE).astype(BF16)
    wuv = _per_head_lanes(mla_w_uv[0], C_HEADS, V_DIM).astype(BF16)
    cos_t, sin_lo, sin_hi = _rope_tables(s)
    q1, k1, v1 = _mla_prep(x1, attn_norm[1], win, mla_q_norm[0], mla_kv_norm[0], wuq, wuk, wuv,
                           cos_t, sin_lo, sin_hi, tm)
    oc = _mla_attn(q1, k1, v1, t5_table)
    out = _mix_ffn(x1, [oc], [odd_w_out[0].astype(BF16)], ffn_norm[1],
                   ffn_w_gate[1].astype(BF16), ffn_w_up[1].astype(BF16), ffn_w_down[1].astype(BF16),
                   final_norm, tm)
    return out.reshape(b, s, d)
```

```python
import functools
import math

import numpy as np
import jax
import jax.numpy as jnp
from jax import lax
from jax.experimental import pallas as pl
from jax.experimental.pallas import tpu as pltpu

F32 = jnp.float32
BF16 = jnp.bfloat16

HEAD_DIM = 64
GRID_W = 64
RMS_EPS = 1e-6
A_HEADS = 8
B_HEADS = 8
DILATED_BRANCHES = ((128, 1), (512, 4), (2048, 16))
NA_ROWS = 8
NA_COLS = 16
C_HEADS = 16
Q_LORA = 384
KV_LORA = 128
QK_NOPE = 64
QK_ROPE = 32
V_DIM = 64
ROPE_THETA = 10000.0
T5_BUCKETS = 32
T5_MAX_DIST = 1024

LOG2E = math.log2(math.e)
NEG = -1e30
LANES = 128
VMEM_LIMIT = 56 * 1024 * 1024


def _t5_upper_bounds():
    half = T5_BUCKETS // 2
    max_exact = half // 2
    n = np.arange(0, 4 * T5_MAX_DIST, dtype=np.int64)
    nf = np.maximum(n, max_exact).astype(np.float32)
    val = (np.log(nf / np.float32(max_exact)) / np.float32(math.log(T5_MAX_DIST / max_exact))
           * np.float32(half - max_exact))
    large = np.minimum(max_exact + val.astype(np.int32), half - 1)
    bucket = np.where(n < max_exact, n, large)
    return tuple(int(np.argmax(bucket > b)) for b in range(half - 1))


T5_UPPER = _t5_upper_bounds()
T5_FAR = T5_UPPER[-1]


def _t5_bias_tile(t5_ref, head, rel):
    half = T5_BUCKETS // 2
    n = jnp.abs(rel)
    vneg = jnp.full(rel.shape, t5_ref[half - 1, head], F32)
    vpos = jnp.full(rel.shape, t5_ref[2 * half - 1, head], F32)
    for b in range(half - 2, -1, -1):
        inb = n < T5_UPPER[b]
        vneg = jnp.where(inb, t5_ref[b, head], vneg)
        vpos = jnp.where(inb, t5_ref[half + b, head], vpos)
    return jnp.where(rel > 0, vpos, vneg) * LOG2E


def _rms(x, g):
    return x * lax.rsqrt(jnp.mean(x * x, axis=-1, keepdims=True) + RMS_EPS) * g


def _nt_dot(a, b):
    return lax.dot_general(a, b, (((1,), (1,)), ((), ())), preferred_element_type=F32)


def _resident(shape, index_map):
    return pl.BlockSpec(shape, index_map, pipeline_mode=pl.Buffered(1))


def _norm_proj_kernel(x_ref, g_ref, w_ref, o_ref):
    h = _rms(x_ref[...], g_ref[...]).astype(BF16)
    o_ref[...] = jnp.dot(h, w_ref[...], preferred_element_type=F32).astype(o_ref.dtype)


def _norm_proj(x, g, w, tm):
    s, d = x.shape
    n = w.shape[1]
    return pl.pallas_call(
        _norm_proj_kernel,
        grid=(s // tm,),
        in_specs=[
            pl.BlockSpec((tm, d), lambda i: (i, 0)),
            _resident((1, d), lambda i: (0, 0)),
            _resident((d, n), lambda i: (0, 0)),
        ],
        out_specs=pl.BlockSpec((tm, n), lambda i: (i, 0)),
        out_shape=jax.ShapeDtypeStruct((s, n), BF16),
        compiler_params=pltpu.CompilerParams(
            dimension_semantics=("parallel",), vmem_limit_bytes=VMEM_LIMIT),
        name="norm_proj",
    )(x, g.reshape(1, d), w)


DIL_SIDE = 64
DIL_REACH = max(w // 2 for w, _ in DILATED_BRANCHES)
DIL_TQ = 256
DIL_W = DIL_TQ + 2 * DIL_REACH


def _dilated_kernel(t5_ref, q_ref, k_ref, v_ref, o_ref, bias_ref, *, seq):
    hp = pl.program_id(0)
    i = pl.program_id(1)

    @pl.when(i == 0)
    def _():
        row = lax.broadcasted_iota(jnp.int32, (DIL_TQ, DIL_W), 0)
        col = lax.broadcasted_iota(jnp.int32, (DIL_TQ, DIL_W), 1)
        rel = col - row - DIL_REACH
        n = jnp.abs(rel)
        count = jnp.zeros(rel.shape, jnp.int32)
        for window, dil in DILATED_BRANCHES:
            member = ((rel & (dil - 1)) == 0) & (n <= (window // 2))
            count = count + member.astype(jnp.int32)
        logmult = jnp.where(count == 3, math.log2(3.0), jnp.where(count == 2, 1.0, 0.0))
        for hh in range(2):
            b = _t5_bias_tile(t5_ref, 2 * hp + hh, rel) + logmult
            bias_ref[hh] = jnp.where(count > 0, b, NEG)

    start = pl.multiple_of(i * DIL_TQ, DIL_TQ)
    kw = k_ref[pl.ds(start, DIL_W), :]
    vw = v_ref[pl.ds(start, DIL_W), :]
    q = q_ref[...]
    lane = lax.broadcasted_iota(jnp.int32, q.shape, 1)
    kpos = start + lax.broadcasted_iota(jnp.int32, (1, DIL_W), 1) - DIL_REACH
    edge = jnp.where((kpos >= 0) & (kpos < seq), 0.0, NEG)

    outs = []
    for hh in range(2):
        in_head = (lane >= hh * HEAD_DIM) & (lane < (hh + 1) * HEAD_DIM)
        qm = jnp.where(in_head, q, jnp.zeros_like(q))
        s = _nt_dot(qm, kw) + bias_ref[hh] + edge
        m = jnp.max(s, axis=-1, keepdims=True)
        p = jnp.exp2(s - m)
        l = jnp.sum(p, axis=-1, keepdims=True)
        o = jnp.dot(p.astype(BF16), vw, preferred_element_type=F32)
        outs.append(o / l)
    o_ref[...] = jnp.where(lane < HEAD_DIM, outs[0], outs[1]).astype(o_ref.dtype)


def _dilated(proj, kpad, vpad, t5_table):
    s = proj.shape[0]
    sp = kpad.shape[0]
    pairs = A_HEADS // 2
    return pl.pallas_call(
        functools.partial(_dilated_kernel, seq=s),
        grid=(pairs, s // DIL_TQ),
        in_specs=[
            pl.BlockSpec(memory_space=pltpu.SMEM),
            pl.BlockSpec((DIL_TQ, LANES), lambda h, i: (i, h)),
            _resident((sp, LANES), lambda h, i: (0, h)),
            _resident((sp, LANES), lambda h, i: (0, h)),
        ],
        out_specs=pl.BlockSpec((DIL_TQ, LANES), lambda h, i: (i, h)),
        out_shape=jax.ShapeDtypeStruct((s, A_HEADS * HEAD_DIM), BF16),
        scratch_shapes=[pltpu.VMEM((2, DIL_TQ, DIL_W), F32)],
        compiler_params=pltpu.CompilerParams(
            dimension_semantics=("arbitrary", "arbitrary"), vmem_limit_bytes=VMEM_LIMIT),
        name="dilated",
    )(t5_table, proj, kpad, vpad)


NA_QROWS = 8
NA_KROWS = 16
NA_TQ = NA_QROWS * GRID_W
NA_TK = NA_KROWS * GRID_W
NA_RO = 2 * NA_ROWS - 1
NA_CO = 2 * NA_COLS - 1


def _natten_kernel(rpb_ref, q_ref, k_ref, v_ref, o_ref, tile_ref, bias_ref, *, rows):
    hp = pl.program_id(0)
    qi = pl.program_id(1)
    nq = pl.num_programs(1)
    rb = qi * NA_QROWS
    kb0 = jnp.clip(rb - NA_ROWS // 2, 0, rows - NA_KROWS)

    @pl.when(qi == 0)
    def _():
        c = lax.broadcasted_iota(jnp.int32, (GRID_W, LANES), 0)
        kc = lax.broadcasted_iota(jnp.int32, (GRID_W, LANES), 1) & (GRID_W - 1)
        co = jnp.clip(kc - c + (NA_COLS - 1), 0, NA_CO - 1)
        c0 = jnp.clip(c - NA_COLS // 2, 0, GRID_W - NA_COLS)
        col_ok = (kc >= c0) & (kc < c0 + NA_COLS)
        for hh in range(2):
            head = 2 * hp + hh

            def body(ro, carry):
                base = (head * NA_RO + ro) * NA_CO
                v = jnp.full(co.shape, rpb_ref[base + NA_CO - 1], F32)
                for k in range(NA_CO - 2, -1, -1):
                    v = jnp.where(co == k, rpb_ref[base + k], v)
                tile_ref[hh, ro] = jnp.where(col_ok, v * LOG2E, NEG)
                return carry

            lax.fori_loop(0, NA_RO, body, 0)

    @pl.when((qi == 0) | (qi == 1) | (qi == nq - 1))
    def _():
        lane = lax.broadcasted_iota(jnp.int32, (GRID_W, LANES), 1)
        left = lane < GRID_W
        for i in range(NA_QROWS):
            r = rb + i
            r0 = jnp.clip(r - NA_ROWS // 2, 0, rows - NA_ROWS)
            for a in range(NA_KROWS // 2):
                kr = kb0 + 2 * a
                ro = kr - r + (NA_ROWS - 1)
                ok0 = (kr >= r0) & (kr < r0 + NA_ROWS)
                ok1 = (kr + 1 >= r0) & (kr + 1 < r0 + NA_ROWS)
                ro0 = jnp.clip(ro, 0, NA_RO - 1)
                ro1 = jnp.clip(ro + 1, 0, NA_RO - 1)
                for hh in range(2):
                    t0 = jnp.where(ok0, tile_ref[hh, ro0], NEG)
                    t1 = jnp.where(ok1, tile_ref[hh, ro1], NEG)
                    bias_ref[hh, i * GRID_W:(i + 1) * GRID_W, a * LANES:(a + 1) * LANES] = (
                        jnp.where(left, t0, t1))

    kstart = pl.multiple_of(kb0 * GRID_W, GRID_W * 4)
    kw = k_ref[pl.ds(kstart, NA_TK), :]
    vw = v_ref[pl.ds(kstart, NA_TK), :]
    q = q_ref[...]
    lane = lax.broadcasted_iota(jnp.int32, q.shape, 1)
    outs = []
    for hh in range(2):
        in_head = (lane >= hh * HEAD_DIM) & (lane < (hh + 1) * HEAD_DIM)
        qm = jnp.where(in_head, q, jnp.zeros_like(q))
        s = _nt_dot(qm, kw) + bias_ref[hh]
        m = jnp.max(s, axis=-1, keepdims=True)
        p = jnp.exp2(s - m)
        l = jnp.sum(p, axis=-1, keepdims=True)
        o = jnp.dot(p.astype(BF16), vw, preferred_element_type=F32)
        outs.append(o / l)
    o_ref[...] = jnp.where(lane < HEAD_DIM, outs[0], outs[1]).astype(o_ref.dtype)


def _natten(proj, rpb_flat, col0):
    s = proj.shape[0]
    rows = s // GRID_W
    pairs = B_HEADS // 2
    return pl.pallas_call(
        functools.partial(_natten_kernel, rows=rows),
        grid=(pairs, s // NA_TQ),
        in_specs=[
            pl.BlockSpec(memory_space=pltpu.SMEM),
            pl.BlockSpec((NA_TQ, LANES), lambda h, i: (i, col0 + h)),
            _resident((s, LANES), lambda h, i: (0, col0 + pairs + h)),
            _resident((s, LANES), lambda h, i: (0, col0 + 2 * pairs + h)),
        ],
        out_specs=pl.BlockSpec((NA_TQ, LANES), lambda h, i: (i, h)),
        out_shape=jax.ShapeDtypeStruct((s, B_HEADS * HEAD_DIM), BF16),
        scratch_shapes=[
            pltpu.VMEM((2, NA_RO, GRID_W, LANES), F32),
            pltpu.VMEM((2, NA_TQ, NA_TK), F32),
        ],
        compiler_params=pltpu.CompilerParams(
            dimension_semantics=("arbitrary", "arbitrary"), vmem_limit_bytes=VMEM_LIMIT),
        name="natten",
    )(rpb_flat, proj, proj, proj)


FF_CHUNK = 256


def _mix_ffn_kernel(*refs, n_o, final):
    x_ref = refs[0]
    o_refs = refs[1:1 + n_o]
    wo_refs = refs[1 + n_o:1 + 2 * n_o]
    g_ref, wg_ref, wu_ref, wd_ref = refs[1 + 2 * n_o:5 + 2 * n_o]
    rest = refs[5 + 2 * n_o:]
    gf_ref = rest[0] if final else None
    out_ref = rest[-1]

    y = x_ref[...]
    for o_ref, wo_ref in zip(o_refs, wo_refs):
        y = y + jnp.dot(o_ref[...], wo_ref[...], preferred_element_type=F32)
    h = _rms(y, g_ref[...]).astype(BF16)
    d_ff = wg_ref.shape[1]
    acc = jnp.zeros(y.shape, F32)
    for c in range(0, d_ff, FF_CHUNK):
        gate = jnp.dot(h, wg_ref[:, c:c + FF_CHUNK], preferred_element_type=F32)
        up = jnp.dot(h, wu_ref[:, c:c + FF_CHUNK], preferred_element_type=F32)
        act = (gate * jax.nn.sigmoid(gate) * up).astype(BF16)
        acc = acc + jnp.dot(act, wd_ref[c:c + FF_CHUNK, :], preferred_element_type=F32)
    y = y + acc
    if final:
        y = _rms(y, gf_ref[...])
    out_ref[...] = y


def _mix_ffn(x, o_parts, wo_parts, g, wg, wu, wd, gf, tm):
    s, d = x.shape
    d_ff = wg.shape[1]
    assert d_ff % FF_CHUNK == 0
    n_o = len(o_parts)
    final = gf is not None
    in_specs = [pl.BlockSpec((tm, d), lambda i: (i, 0))]
    in_specs += [pl.BlockSpec((tm, o.shape[1]), lambda i: (i, 0)) for o in o_parts]
    in_specs += [_resident(w.shape, lambda i: (0, 0)) for w in wo_parts]
    in_specs += [
        _resident((1, d), lambda i: (0, 0)),
        _resident((d, d_ff), lambda i: (0, 0)),
        _resident((d, d_ff), lambda i: (0, 0)),
        _resident((d_ff, d), lambda i: (0, 0)),
    ]
    args = [x, *o_parts, *wo_parts, g.reshape(1, d), wg, wu, wd]
    if final:
        in_specs.append(_resident((1, d), lambda i: (0, 0)))
        args.append(gf.reshape(1, d))
    return pl.pallas_call(
        functools.partial(_mix_ffn_kernel, n_o=n_o, final=final),
        grid=(s // tm,),
        in_specs=in_specs,
        out_specs=pl.BlockSpec((tm, d), lambda i: (i, 0)),
        out_shape=jax.ShapeDtypeStruct((s, d), F32),
        compiler_params=pltpu.CompilerParams(
            dimension_semantics=("parallel",), vmem_limit_bytes=VMEM_LIMIT),
        name="mix_ffn",
    )(*args)


ROPE_LANE0 = QK_NOPE
ROPE_HALF = QK_ROPE // 2


def _rope(x, cos_t, sin_lo, sin_hi):
    return (x * cos_t + pltpu.roll(x, LANES - ROPE_HALF, 1) * sin_lo
            + pltpu.roll(x, ROPE_HALF, 1) * sin_hi)


def _mla_prep_kernel(x_ref, g_ref, win_ref, qn_ref, kvn_ref, wuq_ref, wuk_ref, wuv_ref,
                     cos_ref, slo_ref, shi_ref, q_out, k_out, v_out):
    h = _rms(x_ref[...], g_ref[...]).astype(BF16)
    proj = jnp.dot(h, win_ref[...], preferred_element_type=F32)
    cq = _rms(proj[:, :Q_LORA], qn_ref[...]).astype(BF16)
    ckv = _rms(proj[:, Q_LORA:Q_LORA + KV_LORA], kvn_ref[...]).astype(BF16)
    cos_t, sin_lo, sin_hi = cos_ref[...], slo_ref[...], shi_ref[...]
    kr = _rope(proj[:, Q_LORA + KV_LORA:], cos_t, sin_lo, sin_hi)
    q = jnp.dot(cq, wuq_ref[...], preferred_element_type=F32)
    kn = jnp.dot(ckv, wuk_ref[...], preferred_element_type=F32)
    v = jnp.dot(ckv, wuv_ref[...], preferred_element_type=F32)
    lane = lax.broadcasted_iota(jnp.int32, (1, LANES), 1)
    one_col = jnp.where(lane == V_DIM, 1.0, 0.0)
    for hd in range(C_HEADS):
        sl = slice(hd * LANES, (hd + 1) * LANES)
        q_out[hd] = _rope(q[:, sl], cos_t, sin_lo, sin_hi).astype(BF16)
        k_out[hd] = (kn[:, sl] + kr).astype(BF16)
        v_out[hd] = (v[:, sl] + one_col).astype(BF16)


def _mla_prep(x, g, win, qn, kvn, wuq, wuk, wuv, cos_t, sin_lo, sin_hi, tm):
    s, d = x.shape
    const = lambda i: (0, 0)
    row = lambda i: (i, 0)
    head_out = jax.ShapeDtypeStruct((C_HEADS, s, LANES), BF16)
    head_spec = pl.BlockSpec((C_HEADS, tm, LANES), lambda i: (0, i, 0))
    return pl.pallas_call(
        _mla_prep_kernel,
        grid=(s // tm,),
        in_specs=[
            pl.BlockSpec((tm, d), row),
            _resident((1, d), const),
            _resident(win.shape, const),
            _resident((1, Q_LORA), const),
            _resident((1, KV_LORA), const),
            _resident(wuq.shape, const),
            _resident(wuk.shape, const),
            _resident(wuv.shape, const),
            pl.BlockSpec((tm, LANES), row),
            pl.BlockSpec((tm, LANES), row),
            pl.BlockSpec((tm, LANES), row),
        ],
        out_specs=[head_spec, head_spec, head_spec],
        out_shape=[head_out, head_out, head_out],
        compiler_params=pltpu.CompilerParams(
            dimension_semantics=("parallel",), vmem_limit_bytes=VMEM_LIMIT),
        name="mla_prep",
    )(x, g.reshape(1, d), win, qn.reshape(1, -1), kvn.reshape(1, -1), wuq, wuk, wuv,
      cos_t, sin_lo, sin_hi)


MLA_TQ = 512
MLA_TK = 512
MLA_BAND = (T5_FAR - 1 + MLA_TK - 1) // MLA_TK
MLA_UNROLL = 2


def _mla_attn_kernel(t5_ref, q_ref, k_ref, v_ref, o_ref, bias_ref, *, seq):
    hp = pl.program_id(0)
    qi = pl.program_id(1)
    nk = seq // MLA_TK
    half = T5_BUCKETS // 2
    far = MLA_BAND + 1

    @pl.when(qi == 0)
    def _():
        row = lax.broadcasted_iota(jnp.int32, (MLA_TQ, MLA_TK), 0)
        col = lax.broadcasted_iota(jnp.int32, (MLA_TQ, MLA_TK), 1)
        for hh in range(2):
            head = 2 * hp + hh
            bias_ref[hh, 0] = jnp.full((MLA_TQ, MLA_TK), t5_ref[half - 1, head] * LOG2E, F32)
            bias_ref[hh, 2 * far] = jnp.full(
                (MLA_TQ, MLA_TK), t5_ref[2 * half - 1, head] * LOG2E, F32)
            for d in range(-MLA_BAND, MLA_BAND + 1):
                bias_ref[hh, d + far] = _t5_bias_tile(t5_ref, head, col - row + d * MLA_TK)

    lane = lax.broadcasted_iota(jnp.int32, (MLA_TQ, LANES), 1)
    qs = [q_ref[0], q_ref[1]]

    slab = MLA_UNROLL * MLA_TK

    def step(t, carry):
        kstart = pl.multiple_of(t * slab, slab)
        scores = [_nt_dot(qs[hh], k_ref[hh, pl.ds(kstart, slab), :]) for hh in range(2)]
        anchor = jnp.max(jnp.where(qi < 0, scores[1][:, :LANES], NEG), axis=-1, keepdims=True)
        new = []
        for hh in range(2):
            m, acc = carry[hh]
            s = jnp.concatenate(
                [scores[hh][:, u * MLA_TK:(u + 1) * MLA_TK]
                 + bias_ref[hh, jnp.clip(t * MLA_UNROLL + u - qi, -far, far) + far]
                 for u in range(MLA_UNROLL)], axis=1)
            m_blk = jnp.max(s, axis=-1, keepdims=True)
            if hh == 0:
                m_blk = jnp.maximum(m_blk, anchor)
            m_new = jnp.maximum(m, m_blk)
            alpha = jnp.exp2(m - m_new)
            p = jnp.exp2(s - m_new)
            vt = v_ref[hh, pl.ds(kstart, slab), :]
            acc = alpha * acc + jnp.dot(p.astype(BF16), vt, preferred_element_type=F32)
            new.append((m_new, acc))
        return tuple(new)

    init = (jnp.full((MLA_TQ, 1), NEG, F32), jnp.zeros((MLA_TQ, LANES), F32))
    carry = lax.fori_loop(0, nk // MLA_UNROLL, step, (init, init))
    outs = [acc / acc[:, V_DIM:V_DIM + 1] for _, acc in carry]
    out = jnp.where(lane < V_DIM, outs[0], pltpu.roll(outs[1], V_DIM, 1))
    o_ref[...] = out.astype(o_ref.dtype)


def _mla_attn(q, k, v, t5_table):
    heads, s, _ = q.shape
    pairs = heads // 2
    assert MLA_TQ == MLA_TK and (s // MLA_TK) % MLA_UNROLL == 0
    return pl.pallas_call(
        functools.partial(_mla_attn_kernel, seq=s),
        grid=(pairs, s // MLA_TQ),
        in_specs=[
            pl.BlockSpec(memory_space=pltpu.SMEM),
            pl.BlockSpec((2, MLA_TQ, LANES), lambda h, i: (h, i, 0)),
            _resident((2, s, LANES), lambda h, i: (h, 0, 0)),
            _resident((2, s, LANES), lambda h, i: (h, 0, 0)),
        ],
        out_specs=pl.BlockSpec((MLA_TQ, LANES), lambda h, i: (i, h)),
        out_shape=jax.ShapeDtypeStruct((s, heads * V_DIM), BF16),
        scratch_shapes=[pltpu.VMEM((2, 2 * MLA_BAND + 3, MLA_TQ, MLA_TK), F32)],
        compiler_params=pltpu.CompilerParams(
            dimension_semantics=("arbitrary", "arbitrary"), vmem_limit_bytes=VMEM_LIMIT),
        name="mla_attn",
    )(t5_table, q, k, v)


def _per_head_lanes(w, heads, width, lane0=0):
    k = w.shape[0]
    w = w.reshape(k, heads, width)
    w = jnp.pad(w, ((0, 0), (0, 0), (lane0, LANES - lane0 - width)))
    return w.reshape(k, heads * LANES)


def _rope_tables(s):
    pos = jnp.arange(s, dtype=F32)
    freqs = ROPE_THETA ** (-jnp.arange(0, QK_ROPE, 2, dtype=F32) / QK_ROPE)
    ang = pos[:, None] * freqs[None, :]
    cos, sin = jnp.cos(ang), jnp.sin(ang)
    zeros = jnp.zeros((s, ROPE_HALF), F32)
    tail = LANES - ROPE_LANE0 - QK_ROPE
    cos_t = jnp.concatenate([jnp.ones((s, ROPE_LANE0), F32), cos, cos, jnp.ones((s, tail), F32)], axis=1)
    sin_lo = jnp.concatenate([jnp.zeros((s, ROPE_LANE0), F32), -sin, zeros, jnp.zeros((s, tail), F32)], axis=1)
    sin_hi = jnp.concatenate([jnp.zeros((s, ROPE_LANE0), F32), zeros, sin, jnp.zeros((s, tail), F32)], axis=1)
    return cos_t, sin_lo, sin_hi


def kernel(x, t5_table, attn_norm, ffn_norm, even_w_in, na_rpb, even_w_out, odd_w_in, mla_q_norm,
           mla_w_uq, mla_kv_norm, mla_w_uk, mla_w_uv, odd_w_out, ffn_w_gate, ffn_w_up, ffn_w_down,
           final_norm):
    b, s, d = x.shape
    assert b == 1 and s % (GRID_W * NA_KROWS) == 0 and s % MLA_TQ == 0
    x0 = x.reshape(s, d)
    tm = 512

    a_cols = A_HEADS * HEAD_DIM
    b_cols = B_HEADS * HEAD_DIM
    qscale = HEAD_DIM ** -0.5 * LOG2E
    w_in = even_w_in[0]
    col_scale = jnp.concatenate([
        jnp.full((a_cols,), qscale, F32), jnp.ones((2 * a_cols,), F32),
        jnp.full((b_cols,), qscale, F32), jnp.ones((2 * b_cols,), F32)])
    proj = _norm_proj(x0, attn_norm[0], (w_in * col_scale).astype(BF16), tm)
    pad = ((DIL_REACH, DIL_REACH), (0, 0))
    kpad = jnp.pad(proj[:, a_cols:2 * a_cols], pad)
    vpad = jnp.pad(proj[:, 2 * a_cols:3 * a_cols], pad)
    oa = _dilated(proj, kpad, vpad, t5_table)
    ob = _natten(proj, na_rpb[0].reshape(-1), 3 * a_cols // LANES)
    w_out = even_w_out[0].astype(BF16)
    x1 = _mix_ffn(x0, [oa, ob], [w_out[:a_cols], w_out[a_cols:]], ffn_norm[0],
                  ffn_w_gate[0].astype(BF16), ffn_w_up[0].astype(BF16), ffn_w_down[0].astype(BF16),
                  None, tm)

    cscale = (QK_NOPE + QK_ROPE) ** -0.5 * LOG2E
    w_in1 = odd_w_in[0]
    win = jnp.concatenate([
        w_in1[:, :Q_LORA + KV_LORA],
        _per_head_lanes(w_in1[:, Q_LORA + KV_LORA:], 1, QK_ROPE, ROPE_LANE0)], axis=1).astype(BF16)
    wuq = (_per_head_lanes(mla_w_uq[0], C_HEADS, QK_NOPE + QK_ROPE) * cscale).astype(BF16)
    wuk = _per_head_lanes(mla_w_uk[0], C_HEADS, QK_NOPE).astype(BF16)
    wuv = _per_head_lanes(mla_w_uv[0], C_HEADS, V_DIM).astype(BF16)
    cos_t, sin_lo, sin_hi = _rope_tables(s)
    q1, k1, v1 = _mla_prep(x1, attn_norm[1], win, mla_q_norm[0], mla_kv_norm[0], wuq, wuk, wuv,
                           cos_t, sin_lo, sin_hi, tm)
    oc = _mla_attn(q1, k1, v1, t5_table)
    out = _mix_ffn(x1, [oc], [odd_w_out[0].astype(BF16)], ffn_norm[1],
                   ffn_w_gate[1].astype(BF16), ffn_w_up[1].astype(BF16), ffn_w_down[1].astype(BF16),
                   final_norm, tm)
    return out.reshape(b, s, d)
```

```python
import functools
import math

import numpy as np
import jax
import jax.numpy as jnp
from jax import lax
from jax.experimental import pallas as pl
from jax.experimental.pallas import tpu as pltpu

F32 = jnp.float32
BF16 = jnp.bfloat16

HEAD_DIM = 64
GRID_W = 64
RMS_EPS = 1e-6
A_HEADS = 8
B_HEADS = 8
DILATED_BRANCHES = ((128, 1), (512, 4), (2048, 16))
NA_ROWS = 8
NA_COLS = 16
C_HEADS = 16
Q_LORA = 384
KV_LORA = 128
QK_NOPE = 64
QK_ROPE = 32
V_DIM = 64
ROPE_THETA = 10000.0
T5_BUCKETS = 32
T5_MAX_DIST = 1024

LOG2E = math.log2(math.e)
NEG = -1e30
LANES = 128
VMEM_LIMIT = 56 * 1024 * 1024


def _t5_upper_bounds():
    half = T5_BUCKETS // 2
    max_exact = half // 2
    n = np.arange(0, 4 * T5_MAX_DIST, dtype=np.int64)
    nf = np.maximum(n, max_exact).astype(np.float32)
    val = (np.log(nf / np.float32(max_exact)) / np.float32(math.log(T5_MAX_DIST / max_exact))
           * np.float32(half - max_exact))
    large = np.minimum(max_exact + val.astype(np.int32), half - 1)
    bucket = np.where(n < max_exact, n, large)
    return tuple(int(np.argmax(bucket > b)) for b in range(half - 1))


T5_UPPER = _t5_upper_bounds()
T5_FAR = T5_UPPER[-1]


def _t5_bias_tile(t5_ref, head, rel):
    half = T5_BUCKETS // 2
    n = jnp.abs(rel)
    vneg = jnp.full(rel.shape, t5_ref[half - 1, head], F32)
    vpos = jnp.full(rel.shape, t5_ref[2 * half - 1, head], F32)
    for b in range(half - 2, -1, -1):
        inb = n < T5_UPPER[b]
        vneg = jnp.where(inb, t5_ref[b, head], vneg)
        vpos = jnp.where(inb, t5_ref[half + b, head], vpos)
    return jnp.where(rel > 0, vpos, vneg) * LOG2E


def _rms(x, g):
    return x * lax.rsqrt(jnp.mean(x * x, axis=-1, keepdims=True) + RMS_EPS) * g


def _nt_dot(a, b):
    return lax.dot_general(a, b, (((1,), (1,)), ((), ())), preferred_element_type=F32)


def _resident(shape, index_map):
    return pl.BlockSpec(shape, index_map, pipeline_mode=pl.Buffered(1))


def _norm_proj_kernel(x_ref, g_ref, w_ref, o_ref):
    h = _rms(x_ref[...], g_ref[...]).astype(BF16)
    o_ref[...] = jnp.dot(h, w_ref[...], preferred_element_type=F32).astype(o_ref.dtype)


def _norm_proj(x, g, w, tm):
    s, d = x.shape
    n = w.shape[1]
    return pl.pallas_call(
        _norm_proj_kernel,
        grid=(s // tm,),
        in_specs=[
            pl.BlockSpec((tm, d), lambda i: (i, 0)),
            _resident((1, d), lambda i: (0, 0)),
            _resident((d, n), lambda i: (0, 0)),
        ],
        out_specs=pl.BlockSpec((tm, n), lambda i: (i, 0)),
        out_shape=jax.ShapeDtypeStruct((s, n), BF16),
        compiler_params=pltpu.CompilerParams(
            dimension_semantics=("parallel",), vmem_limit_bytes=VMEM_LIMIT),
        name="norm_proj",
    )(x, g.reshape(1, d), w)


DIL_SIDE = 64
DIL_REACH = max(w // 2 for w, _ in DILATED_BRANCHES)
DIL_TQ = 256
DIL_W = DIL_TQ + 2 * DIL_REACH


def _dilated_kernel(t5_ref, q_ref, k_ref, v_ref, o_ref, bias_ref, *, seq):
    hp = pl.program_id(0)
    i = pl.program_id(1)

    @pl.when(i == 0)
    def _():
        row = lax.broadcasted_iota(jnp.int32, (DIL_TQ, DIL_W), 0)
        col = lax.broadcasted_iota(jnp.int32, (DIL_TQ, DIL_W), 1)
        rel = col - row - DIL_REACH
        n = jnp.abs(rel)
        count = jnp.zeros(rel.shape, jnp.int32)
        for window, dil in DILATED_BRANCHES:
            member = ((rel & (dil - 1)) == 0) & (n <= (window // 2))
            count = count + member.astype(jnp.int32)
        logmult = jnp.where(count == 3, math.log2(3.0), jnp.where(count == 2, 1.0, 0.0))
        for hh in range(2):
            b = _t5_bias_tile(t5_ref, 2 * hp + hh, rel) + logmult
            bias_ref[hh] = jnp.where(count > 0, b, NEG)

    start = pl.multiple_of(i * DIL_TQ, DIL_TQ)
    kw = k_ref[pl.ds(start, DIL_W), :]
    vw = v_ref[pl.ds(start, DIL_W), :]
    q = q_ref[...]
    lane = lax.broadcasted_iota(jnp.int32, q.shape, 1)
    kpos = start + lax.broadcasted_iota(jnp.int32, (1, DIL_W), 1) - DIL_REACH
    edge = jnp.where((kpos >= 0) & (kpos < seq), 0.0, NEG)

    outs = []
    for hh in range(2):
        in_head = (lane >= hh * HEAD_DIM) & (lane < (hh + 1) * HEAD_DIM)
        qm = jnp.where(in_head, q, jnp.zeros_like(q))
        s = _nt_dot(qm, kw) + bias_ref[hh] + edge
        m = jnp.max(s, axis=-1, keepdims=True)
        p = jnp.exp2(s - m)
        l = jnp.sum(p, axis=-1, keepdims=True)
        o = jnp.dot(p.astype(BF16), vw, preferred_element_type=F32)
        outs.append(o / l)
    o_ref[...] = jnp.where(lane < HEAD_DIM, outs[0], outs[1]).astype(o_ref.dtype)


def _dilated(proj, kpad, vpad, t5_table):
    s = proj.shape[0]
    sp = kpad.shape[0]
    pairs = A_HEADS // 2
    return pl.pallas_call(
        functools.partial(_dilated_kernel, seq=s),
        grid=(pairs, s // DIL_TQ),
        in_specs=[
            pl.BlockSpec(memory_space=pltpu.SMEM),
            pl.BlockSpec((DIL_TQ, LANES), lambda h, i: (i, h)),
            _resident((sp, LANES), lambda h, i: (0, h)),
            _resident((sp, LANES), lambda h, i: (0, h)),
        ],
        out_specs=pl.BlockSpec((DIL_TQ, LANES), lambda h, i: (i, h)),
        out_shape=jax.ShapeDtypeStruct((s, A_HEADS * HEAD_DIM), BF16),
        scratch_shapes=[pltpu.VMEM((2, DIL_TQ, DIL_W), F32)],
        compiler_params=pltpu.CompilerParams(
            dimension_semantics=("arbitrary", "arbitrary"), vmem_limit_bytes=VMEM_LIMIT),
        name="dilated",
    )(t5_table, proj, kpad, vpad)


NA_QROWS = 8
NA_KROWS = 16
NA_TQ = NA_QROWS * GRID_W
NA_TK = NA_KROWS * GRID_W
NA_RO = 2 * NA_ROWS - 1
NA_CO = 2 * NA_COLS - 1


def _natten_kernel(rpb_ref, q_ref, k_ref, v_ref, o_ref, tile_ref, bias_ref, *, rows):
    hp = pl.program_id(0)
    qi = pl.program_id(1)
    nq = pl.num_programs(1)
    rb = qi * NA_QROWS
    kb0 = jnp.clip(rb - NA_ROWS // 2, 0, rows - NA_KROWS)

    @pl.when(qi == 0)
    def _():
        c = lax.broadcasted_iota(jnp.int32, (GRID_W, LANES), 0)
        kc = lax.broadcasted_iota(jnp.int32, (GRID_W, LANES), 1) & (GRID_W - 1)
        co = jnp.clip(kc - c + (NA_COLS - 1), 0, NA_CO - 1)
        c0 = jnp.clip(c - NA_COLS // 2, 0, GRID_W - NA_COLS)
        col_ok = (kc >= c0) & (kc < c0 + NA_COLS)
        for hh in range(2):
            head = 2 * hp + hh

            def body(ro, carry):
                base = (head * NA_RO + ro) * NA_CO
                v = jnp.full(co.shape, rpb_ref[base + NA_CO - 1], F32)
                for k in range(NA_CO - 2, -1, -1):
                    v = jnp.where(co == k, rpb_ref[base + k], v)
                tile_ref[hh, ro] = jnp.where(col_ok, v * LOG2E, NEG)
                return carry

            lax.fori_loop(0, NA_RO, body, 0)

    @pl.when((qi == 0) | (qi == 1) | (qi == nq - 1))
    def _():
        lane = lax.broadcasted_iota(jnp.int32, (GRID_W, LANES), 1)
        left = lane < GRID_W
        for i in range(NA_QROWS):
            r = rb + i
            r0 = jnp.clip(r - NA_ROWS // 2, 0, rows - NA_ROWS)
            for a in range(NA_KROWS // 2):
                kr = kb0 + 2 * a
                ro = kr - r + (NA_ROWS - 1)
                ok0 = (kr >= r0) & (kr < r0 + NA_ROWS)
                ok1 = (kr + 1 >= r0) & (kr + 1 < r0 + NA_ROWS)
                ro0 = jnp.clip(ro, 0, NA_RO - 1)
                ro1 = jnp.clip(ro + 1, 0, NA_RO - 1)
                for hh in range(2):
                    t0 = jnp.where(ok0, tile_ref[hh, ro0], NEG)
                    t1 = jnp.where(ok1, tile_ref[hh, ro1], NEG)
                    bias_ref[hh, i * GRID_W:(i + 1) * GRID_W, a * LANES:(a + 1) * LANES] = (
                        jnp.where(left, t0, t1))

    kstart = pl.multiple_of(kb0 * GRID_W, GRID_W * 4)
    kw = k_ref[pl.ds(kstart, NA_TK), :]
    vw = v_ref[pl.ds(kstart, NA_TK), :]
    q = q_ref[...]
    lane = lax.broadcasted_iota(jnp.int32, q.shape, 1)
    outs = []
    for hh in range(2):
        in_head = (lane >= hh * HEAD_DIM) & (lane < (hh + 1) * HEAD_DIM)
        qm = jnp.where(in_head, q, jnp.zeros_like(q))
        s = _nt_dot(qm, kw) + bias_ref[hh]
        m = jnp.max(s, axis=-1, keepdims=True)
        p = jnp.exp2(s - m)
        l = jnp.sum(p, axis=-1, keepdims=True)
        o = jnp.dot(p.astype(BF16), vw, preferred_element_type=F32)
        outs.append(o / l)
    o_ref[...] = jnp.where(lane < HEAD_DIM, outs[0], outs[1]).astype(o_ref.dtype)


def _natten(proj, rpb_flat, col0):
    s = proj.shape[0]
    rows = s // GRID_W
    pairs = B_HEADS // 2
    return pl.pallas_call(
        functools.partial(_natten_kernel, rows=rows),
        grid=(pairs, s // NA_TQ),
        in_specs=[
            pl.BlockSpec(memory_space=pltpu.SMEM),
            pl.BlockSpec((NA_TQ, LANES), lambda h, i: (i, col0 + h)),
            _resident((s, LANES), lambda h, i: (0, col0 + pairs + h)),
            _resident((s, LANES), lambda h, i: (0, col0 + 2 * pairs + h)),
        ],
        out_specs=pl.BlockSpec((NA_TQ, LANES), lambda h, i: (i, h)),
        out_shape=jax.ShapeDtypeStruct((s, B_HEADS * HEAD_DIM), BF16),
        scratch_shapes=[
            pltpu.VMEM((2, NA_RO, GRID_W, LANES), F32),
            pltpu.VMEM((2, NA_TQ, NA_TK), F32),
        ],
        compiler_params=pltpu.CompilerParams(
            dimension_semantics=("arbitrary", "arbitrary"), vmem_limit_bytes=VMEM_LIMIT),
        name="natten",
    )(rpb_flat, proj, proj, proj)


FF_CHUNK = 256


def _mix_ffn_kernel(*refs, n_o, final):
    x_ref = refs[0]
    o_refs = refs[1:1 + n_o]
    wo_refs = refs[1 + n_o:1 + 2 * n_o]
    g_ref, wg_ref, wu_ref, wd_ref = refs[1 + 2 * n_o:5 + 2 * n_o]
    rest = refs[5 + 2 * n_o:]
    gf_ref = rest[0] if final else None
    out_ref = rest[-1]

    y = x_ref[...]
    for o_ref, wo_ref in zip(o_refs, wo_refs):
        y = y + jnp.dot(o_ref[...], wo_ref[...], preferred_element_type=F32)
    h = _rms(y, g_ref[...]).astype(BF16)
    d_ff = wg_ref.shape[1]
    acc = jnp.zeros(y.shape, F32)
    for c in range(0, d_ff, FF_CHUNK):
        gate = jnp.dot(h, wg_ref[:, c:c + FF_CHUNK], preferred_element_type=F32)
        up = jnp.dot(h, wu_ref[:, c:c + FF_CHUNK], preferred_element_type=F32)
        act = (gate * jax.nn.sigmoid(gate) * up).astype(BF16)
        acc = acc + jnp.dot(act, wd_ref[c:c + FF_CHUNK, :], preferred_element_type=F32)
    y = y + acc
    if final:
        y = _rms(y, gf_ref[...])
    out_ref[...] = y


def _mix_ffn(x, o_parts, wo_parts, g, wg, wu, wd, gf, tm):
    s, d = x.shape
    d_ff = wg.shape[1]
    assert d_ff % FF_CHUNK == 0
    n_o = len(o_parts)
    final = gf is not None
    in_specs = [pl.BlockSpec((tm, d), lambda i: (i, 0))]
    in_specs += [pl.BlockSpec((tm, o.shape[1]), lambda i: (i, 0)) for o in o_parts]
    in_specs += [_resident(w.shape, lambda i: (0, 0)) for w in wo_parts]
    in_specs += [
        _resident((1, d), lambda i: (0, 0)),
        _resident((d, d_ff), lambda i: (0, 0)),
        _resident((d, d_ff), lambda i: (0, 0)),
        _resident((d_ff, d), lambda i: (0, 0)),
    ]
    args = [x, *o_parts, *wo_parts, g.reshape(1, d), wg, wu, wd]
    if final:
        in_specs.append(_resident((1, d), lambda i: (0, 0)))
        args.append(gf.reshape(1, d))
    return pl.pallas_call(
        functools.partial(_mix_ffn_kernel, n_o=n_o, final=final),
        grid=(s // tm,),
        in_specs=in_specs,
        out_specs=pl.BlockSpec((tm, d), lambda i: (i, 0)),
        out_shape=jax.ShapeDtypeStruct((s, d), F32),
        compiler_params=pltpu.CompilerParams(
            dimension_semantics=("parallel",), vmem_limit_bytes=VMEM_LIMIT),
        name="mix_ffn",
    )(*args)


ROPE_LANE0 = QK_NOPE
ROPE_HALF = QK_ROPE // 2


def _rope(x, cos_t, sin_lo, sin_hi):
    return (x * cos_t + pltpu.roll(x, LANES - ROPE_HALF, 1) * sin_lo
            + pltpu.roll(x, ROPE_HALF, 1) * sin_hi)


def _mla_prep_kernel(x_ref, g_ref, win_ref, qn_ref, kvn_ref, wuq_ref, wuk_ref, wuv_ref,
                     cos_ref, slo_ref, shi_ref, q_out, k_out, v_out):
    h = _rms(x_ref[...], g_ref[...]).astype(BF16)
    proj = jnp.dot(h, win_ref[...], preferred_element_type=F32)
    cq = _rms(proj[:, :Q_LORA], qn_ref[...]).astype(BF16)
    ckv = _rms(proj[:, Q_LORA:Q_LORA + KV_LORA], kvn_ref[...]).astype(BF16)
    cos_t, sin_lo, sin_hi = cos_ref[...], slo_ref[...], shi_ref[...]
    kr = _rope(proj[:, Q_LORA + KV_LORA:], cos_t, sin_lo, sin_hi)
    q = jnp.dot(cq, wuq_ref[...], preferred_element_type=F32)
    kn = jnp.dot(ckv, wuk_ref[...], preferred_element_type=F32)
    v = jnp.dot(ckv, wuv_ref[...], preferred_element_type=F32)
    lane = lax.broadcasted_iota(jnp.int32, (1, LANES), 1)
    one_col = jnp.where(lane == V_DIM, 1.0, 0.0)
    for hd in range(C_HEADS):
        sl = slice(hd * LANES, (hd + 1) * LANES)
        q_out[hd] = _rope(q[:, sl], cos_t, sin_lo, sin_hi).astype(BF16)
        k_out[hd] = (kn[:, sl] + kr).astype(BF16)
        v_out[hd] = (v[:, sl] + one_col).astype(BF16)


def _mla_prep(x, g, win, qn, kvn, wuq, wuk, wuv, cos_t, sin_lo, sin_hi, tm):
    s, d = x.shape
    const = lambda i: (0, 0)
    row = lambda i: (i, 0)
    head_out = jax.ShapeDtypeStruct((C_HEADS, s, LANES), BF16)
    head_spec = pl.BlockSpec((C_HEADS, tm, LANES), lambda i: (0, i, 0))
    return pl.pallas_call(
        _mla_prep_kernel,
        grid=(s // tm,),
        in_specs=[
            pl.BlockSpec((tm, d), row),
            _resident((1, d), const),
            _resident(win.shape, const),
            _resident((1, Q_LORA), const),
            _resident((1, KV_LORA), const),
            _resident(wuq.shape, const),
            _resident(wuk.shape, const),
            _resident(wuv.shape, const),
            pl.BlockSpec((tm, LANES), row),
            pl.BlockSpec((tm, LANES), row),
            pl.BlockSpec((tm, LANES), row),
        ],
        out_specs=[head_spec, head_spec, head_spec],
        out_shape=[head_out, head_out, head_out],
        compiler_params=pltpu.CompilerParams(
            dimension_semantics=("parallel",), vmem_limit_bytes=VMEM_LIMIT),
        name="mla_prep",
    )(x, g.reshape(1, d), win, qn.reshape(1, -1), kvn.reshape(1, -1), wuq, wuk, wuv,
      cos_t, sin_lo, sin_hi)


MLA_TQ = 512
MLA_TK = 512
MLA_SLAB = 1024
MLA_ROWS = 32
MLA_BAND = (T5_FAR - 1 + MLA_TK - 1) // MLA_TK
MLA_FAR = MLA_BAND + 1


def _mla_attn_kernel(t5_ref, q_ref, k_ref, v_ref, o_ref,
                     bias_ref, s_ref, p_ref, m_ref, acc_ref, *, seq):
    hp = pl.program_id(0)
    qi = pl.program_id(1)
    n_slab = seq // MLA_SLAB
    tiles = MLA_SLAB // MLA_TK
    half = T5_BUCKETS // 2
    far_neg = [t5_ref[half - 1, 2 * hp + hh] * LOG2E for hh in range(2)]
    far_pos = [t5_ref[2 * half - 1, 2 * hp + hh] * LOG2E for hh in range(2)]

    @pl.when(qi == 0)
    def _():
        row = lax.broadcasted_iota(jnp.int32, (MLA_TQ, MLA_TK), 0)
        col = lax.broadcasted_iota(jnp.int32, (MLA_TQ, MLA_TK), 1)
        for hh in range(2):
            bias_ref[hh, 0] = jnp.full((MLA_TQ, MLA_TK), far_neg[hh], F32)
            bias_ref[hh, 2 * MLA_FAR] = jnp.full((MLA_TQ, MLA_TK), far_pos[hh], F32)
            for d in range(-MLA_BAND, MLA_BAND + 1):
                bias_ref[hh, d + MLA_FAR] = _t5_bias_tile(
                    t5_ref, 2 * hp + hh, col - row + d * MLA_TK)

    m_ref[...] = jnp.full(m_ref.shape, NEG, F32)
    acc_ref[...] = jnp.zeros(acc_ref.shape, F32)

    def scores(slab, slot):
        kstart = pl.multiple_of(slab * MLA_SLAB, MLA_SLAB)
        for hh in range(2):
            s_ref[slot, hh] = _nt_dot(q_ref[hh], k_ref[hh, pl.ds(kstart, MLA_SLAB), :])

    def softmax_pv(slab, slot, banded):
        kstart = pl.multiple_of(slab * MLA_SLAB, MLA_SLAB)
        for hh in range(2):
            if banded:
                idx = [jnp.clip(slab * tiles + u - qi, -MLA_FAR, MLA_FAR) + MLA_FAR
                       for u in range(tiles)]
            else:
                c = jnp.where(slab * tiles < qi, far_neg[hh], far_pos[hh])
            alphas = []
            for r0 in range(0, MLA_TQ, MLA_ROWS):
                rows = slice(r0, r0 + MLA_ROWS)
                s = s_ref[slot, hh, rows, :]
                if banded:
                    s = s + jnp.concatenate(
                        [bias_ref[hh, idx[u], rows, :] for u in range(tiles)], axis=1)
                    m_blk = jnp.max(s, axis=-1, keepdims=True)
                else:
                    m_blk = jnp.max(s, axis=-1, keepdims=True) + c
                m_old = m_ref[hh, rows, :]
                m_new = jnp.maximum(m_old, m_blk)
                m_ref[hh, rows, :] = m_new
                alphas.append(jnp.exp2(m_old - m_new))
                shift = m_new if banded else m_new - c
                p_ref[hh, rows, :] = jnp.exp2(s - shift).astype(BF16)
            alpha = jnp.concatenate(alphas, axis=0)
            acc_ref[hh] = alpha * acc_ref[hh] + jnp.dot(
                p_ref[hh], v_ref[hh, pl.ds(kstart, MLA_SLAB), :], preferred_element_type=F32)

    def is_banded(slab):
        d0 = slab * tiles - qi
        return (d0 >= -MLA_BAND - (tiles - 1)) & (d0 <= MLA_BAND)

    def stage(slab, slot, has_next):
        banded = is_banded(slab)

        def body(with_tiles):
            if has_next:
                scores(slab + 1, 1 - slot)
            softmax_pv(slab, slot, with_tiles)

        pl.when(banded)(functools.partial(body, True))
        pl.when(jnp.logical_not(banded))(functools.partial(body, False))

    def stage_pair(slab, has_next):
        any_banded = is_banded(slab) | is_banded(slab + 1)

        @pl.when(jnp.logical_not(any_banded))
        def _():
            scores(slab + 1, 1)
            softmax_pv(slab, 0, False)
            if has_next:
                scores(slab + 2, 0)
            softmax_pv(slab + 1, 1, False)

        @pl.when(any_banded)
        def _():
            stage(slab, 0, True)
            stage(slab + 1, 1, has_next)

    scores(0, 0)

    def loop_body(t, carry):
        stage_pair(2 * t, True)
        return carry

    lax.fori_loop(0, n_slab // 2 - 1, loop_body, 0)
    stage_pair(n_slab - 2, False)

    lane = lax.broadcasted_iota(jnp.int32, (MLA_TQ, LANES), 1)
    outs = [acc_ref[hh] / acc_ref[hh, :, V_DIM:V_DIM + 1] for hh in range(2)]
    out = jnp.where(lane < V_DIM, outs[0], pltpu.roll(outs[1], V_DIM, 1))
    o_ref[...] = out.astype(o_ref.dtype)


def _mla_attn(q, k, v, t5_table):
    heads, s, _ = q.shape
    pairs = heads // 2
    assert MLA_TQ == MLA_TK and s % (2 * MLA_SLAB) == 0 and MLA_SLAB % MLA_TK == 0
    return pl.pallas_call(
        functools.partial(_mla_attn_kernel, seq=s),
        grid=(pairs, s // MLA_TQ),
        in_specs=[
            pl.BlockSpec(memory_space=pltpu.SMEM),
            pl.BlockSpec((2, MLA_TQ, LANES), lambda h, i: (h, i, 0)),
            _resident((2, s, LANES), lambda h, i: (h, 0, 0)),
            _resident((2, s, LANES), lambda h, i: (h, 0, 0)),
        ],
        out_specs=pl.BlockSpec((MLA_TQ, LANES), lambda h, i: (i, h)),
        out_shape=jax.ShapeDtypeStruct((s, heads * V_DIM), BF16),
        scratch_shapes=[
            pltpu.VMEM((2, 2 * MLA_FAR + 1, MLA_TQ, MLA_TK), F32),
            pltpu.VMEM((2, 2, MLA_TQ, MLA_SLAB), F32),
            pltpu.VMEM((2, MLA_TQ, MLA_SLAB), BF16),
            pltpu.VMEM((2, MLA_TQ, 1), F32),
            pltpu.VMEM((2, MLA_TQ, LANES), F32),
        ],
        compiler_params=pltpu.CompilerParams(
            dimension_semantics=("arbitrary", "arbitrary"), vmem_limit_bytes=VMEM_LIMIT),
        name="mla_attn",
    )(t5_table, q, k, v)


def _per_head_lanes(w, heads, width, lane0=0):
    k = w.shape[0]
    w = w.reshape(k, heads, width)
    w = jnp.pad(w, ((0, 0), (0, 0), (lane0, LANES - lane0 - width)))
    return w.reshape(k, heads * LANES)


def _rope_tables(s):
    pos = jnp.arange(s, dtype=F32)
    freqs = ROPE_THETA ** (-jnp.arange(0, QK_ROPE, 2, dtype=F32) / QK_ROPE)
    ang = pos[:, None] * freqs[None, :]
    cos, sin = jnp.cos(ang), jnp.sin(ang)
    zeros = jnp.zeros((s, ROPE_HALF), F32)
    tail = LANES - ROPE_LANE0 - QK_ROPE
    cos_t = jnp.concatenate([jnp.ones((s, ROPE_LANE0), F32), cos, cos, jnp.ones((s, tail), F32)], axis=1)
    sin_lo = jnp.concatenate([jnp.zeros((s, ROPE_LANE0), F32), -sin, zeros, jnp.zeros((s, tail), F32)], axis=1)
    sin_hi = jnp.concatenate([jnp.zeros((s, ROPE_LANE0), F32), zeros, sin, jnp.zeros((s, tail), F32)], axis=1)
    return cos_t, sin_lo, sin_hi


def kernel(x, t5_table, attn_norm, ffn_norm, even_w_in, na_rpb, even_w_out, odd_w_in, mla_q_norm,
           mla_w_uq, mla_kv_norm, mla_w_uk, mla_w_uv, odd_w_out, ffn_w_gate, ffn_w_up, ffn_w_down,
           final_norm):
    b, s, d = x.shape
    assert b == 1 and s % (GRID_W * NA_KROWS) == 0 and s % MLA_TQ == 0
    x0 = x.reshape(s, d)
    tm = 512

    a_cols = A_HEADS * HEAD_DIM
    b_cols = B_HEADS * HEAD_DIM
    qscale = HEAD_DIM ** -0.5 * LOG2E
    w_in = even_w_in[0]
    col_scale = jnp.concatenate([
        jnp.full((a_cols,), qscale, F32), jnp.ones((2 * a_cols,), F32),
        jnp.full((b_cols,), qscale, F32), jnp.ones((2 * b_cols,), F32)])
    proj = _norm_proj(x0, attn_norm[0], (w_in * col_scale).astype(BF16), tm)
    pad = ((DIL_REACH, DIL_REACH), (0, 0))
    kpad = jnp.pad(proj[:, a_cols:2 * a_cols], pad)
    vpad = jnp.pad(proj[:, 2 * a_cols:3 * a_cols], pad)
    oa = _dilated(proj, kpad, vpad, t5_table)
    ob = _natten(proj, na_rpb[0].reshape(-1), 3 * a_cols // LANES)
    w_out = even_w_out[0].astype(BF16)
    x1 = _mix_ffn(x0, [oa, ob], [w_out[:a_cols], w_out[a_cols:]], ffn_norm[0],
                  ffn_w_gate[0].astype(BF16), ffn_w_up[0].astype(BF16), ffn_w_down[0].astype(BF16),
                  None, tm)

    cscale = (QK_NOPE + QK_ROPE) ** -0.5 * LOG2E
    w_in1 = odd_w_in[0]
    win = jnp.concatenate([
        w_in1[:, :Q_LORA + KV_LORA],
        _per_head_lanes(w_in1[:, Q_LORA + KV_LORA:], 1, QK_ROPE, ROPE_LANE0)], axis=1).astype(BF16)
    wuq = (_per_head_lanes(mla_w_uq[0], C_HEADS, QK_NOPE + QK_ROPE) * cscale).astype(BF16)
    wuk = _per_head_lanes(mla_w_uk[0], C_HEADS, QK_NOPE).astype(BF16)
    wuv = _per_head_lanes(mla_w_uv[0], C_HEADS, V_DIM).astype(BF16)
    cos_t, sin_lo, sin_hi = _rope_tables(s)
    q1, k1, v1 = _mla_prep(x1, attn_norm[1], win, mla_q_norm[0], mla_kv_norm[0], wuq, wuk, wuv,
                           cos_t, sin_lo, sin_hi, tm)
    oc = _mla_attn(q1, k1, v1, t5_table)
    out = _mix_ffn(x1, [oc], [odd_w_out[0].astype(BF16)], ffn_norm[1],
                   ffn_w_gate[1].astype(BF16), ffn_w_up[1].astype(BF16), ffn_w_down[1].astype(BF16),
                   final_norm, tm)
    return out.reshape(b, s, d)
```

```python
import functools
import math

import numpy as np
import jax
import jax.numpy as jnp
from jax import lax
from jax.experimental import pallas as pl
from jax.experimental.pallas import tpu as pltpu

F32 = jnp.float32
BF16 = jnp.bfloat16

HEAD_DIM = 64
GRID_W = 64
RMS_EPS = 1e-6
A_HEADS = 8
B_HEADS = 8
DILATED_BRANCHES = ((128, 1), (512, 4), (2048, 16))
NA_ROWS = 8
NA_COLS = 16
C_HEADS = 16
Q_LORA = 384
KV_LORA = 128
QK_NOPE = 64
QK_ROPE = 32
V_DIM = 64
ROPE_THETA = 10000.0
T5_BUCKETS = 32
T5_MAX_DIST = 1024

LOG2E = math.log2(math.e)
NEG = -1e30
LANES = 128
VMEM_LIMIT = 56 * 1024 * 1024


def _t5_upper_bounds():
    half = T5_BUCKETS // 2
    max_exact = half // 2
    n = np.arange(0, 4 * T5_MAX_DIST, dtype=np.int64)
    nf = np.maximum(n, max_exact).astype(np.float32)
    val = (np.log(nf / np.float32(max_exact)) / np.float32(math.log(T5_MAX_DIST / max_exact))
           * np.float32(half - max_exact))
    large = np.minimum(max_exact + val.astype(np.int32), half - 1)
    bucket = np.where(n < max_exact, n, large)
    return tuple(int(np.argmax(bucket > b)) for b in range(half - 1))


T5_UPPER = _t5_upper_bounds()
T5_FAR = T5_UPPER[-1]


def _t5_bias_tile(t5_ref, head, rel):
    half = T5_BUCKETS // 2
    n = jnp.abs(rel)
    vneg = jnp.full(rel.shape, t5_ref[half - 1, head], F32)
    vpos = jnp.full(rel.shape, t5_ref[2 * half - 1, head], F32)
    for b in range(half - 2, -1, -1):
        inb = n < T5_UPPER[b]
        vneg = jnp.where(inb, t5_ref[b, head], vneg)
        vpos = jnp.where(inb, t5_ref[half + b, head], vpos)
    return jnp.where(rel > 0, vpos, vneg) * LOG2E


def _rms(x, g):
    return x * lax.rsqrt(jnp.mean(x * x, axis=-1, keepdims=True) + RMS_EPS) * g


def _nt_dot(a, b):
    return lax.dot_general(a, b, (((1,), (1,)), ((), ())), preferred_element_type=F32)


def _resident(shape, index_map):
    return pl.BlockSpec(shape, index_map, pipeline_mode=pl.Buffered(1))


def _norm_proj_kernel(x_ref, g_ref, w_ref, o_ref):
    h = _rms(x_ref[...], g_ref[...]).astype(BF16)
    o_ref[...] = jnp.dot(h, w_ref[...], preferred_element_type=F32).astype(o_ref.dtype)


def _norm_proj(x, g, w, tm):
    s, d = x.shape
    n = w.shape[1]
    return pl.pallas_call(
        _norm_proj_kernel,
        grid=(s // tm,),
        in_specs=[
            pl.BlockSpec((tm, d), lambda i: (i, 0)),
            _resident((1, d), lambda i: (0, 0)),
            _resident((d, n), lambda i: (0, 0)),
        ],
        out_specs=pl.BlockSpec((tm, n), lambda i: (i, 0)),
        out_shape=jax.ShapeDtypeStruct((s, n), BF16),
        compiler_params=pltpu.CompilerParams(
            dimension_semantics=("parallel",), vmem_limit_bytes=VMEM_LIMIT),
        name="norm_proj",
    )(x, g.reshape(1, d), w)


DIL_REACH = max(w // 2 for w, _ in DILATED_BRANCHES)
DIL_TQ = 256
DIL_W = DIL_TQ + 2 * DIL_REACH
DIL_BT = 256
DIL_WT = DIL_W // DIL_BT
DIL_DMAX = -(-(DIL_REACH + 1) // DIL_BT)


def _dilated_kernel(t5_ref, q_ref, k_ref, v_ref, o_ref, bias_ref, *, seq):
    hp = pl.program_id(0)
    i = pl.program_id(1)

    @pl.when(i == 0)
    def _():
        row = lax.broadcasted_iota(jnp.int32, (DIL_TQ, DIL_BT), 0)
        col = lax.broadcasted_iota(jnp.int32, (DIL_TQ, DIL_BT), 1)
        for d in range(-DIL_DMAX, DIL_DMAX + 1):
            rel = col - row + d * DIL_BT
            n = jnp.abs(rel)
            count = jnp.zeros(rel.shape, jnp.int32)
            for window, dil in DILATED_BRANCHES:
                member = ((rel & (dil - 1)) == 0) & (n <= (window // 2))
                count = count + member.astype(jnp.int32)
            logmult = jnp.where(count == 3, math.log2(3.0), jnp.where(count == 2, 1.0, 0.0))
            for hh in range(2):
                b = _t5_bias_tile(t5_ref, 2 * hp + hh, rel) + logmult
                bias_ref[hh, d + DIL_DMAX] = jnp.where(count > 0, b, NEG)

    t0 = i * DIL_TQ
    start = pl.multiple_of(jnp.clip(t0 - DIL_REACH, 0, seq - DIL_W), DIL_BT)
    d0 = (start - t0) // DIL_BT
    kw = k_ref[pl.ds(start, DIL_W), :]
    vw = v_ref[pl.ds(start, DIL_W), :]
    q = q_ref[...]
    lane = lax.broadcasted_iota(jnp.int32, q.shape, 1)

    outs = []
    for hh in range(2):
        in_head = (lane >= hh * HEAD_DIM) & (lane < (hh + 1) * HEAD_DIM)
        qm = jnp.where(in_head, q, jnp.zeros_like(q))
        bias = jnp.concatenate(
            [bias_ref[hh, jnp.clip(d0 + k, -DIL_DMAX, DIL_DMAX) + DIL_DMAX]
             for k in range(DIL_WT)], axis=1)
        s = _nt_dot(qm, kw) + bias
        m = jnp.max(s, axis=-1, keepdims=True)
        p = jnp.exp2(s - m)
        l = jnp.sum(p, axis=-1, keepdims=True)
        o = jnp.dot(p.astype(BF16), vw, preferred_element_type=F32)
        outs.append(o / l)
    o_ref[...] = jnp.where(lane < HEAD_DIM, outs[0], outs[1]).astype(o_ref.dtype)


def _dilated(proj, t5_table):
    s = proj.shape[0]
    pairs = A_HEADS // 2
    assert s % DIL_BT == 0 and s >= DIL_W and DIL_W % DIL_BT == 0 and DIL_REACH % DIL_BT == 0
    return pl.pallas_call(
        functools.partial(_dilated_kernel, seq=s),
        grid=(pairs, s // DIL_TQ),
        in_specs=[
            pl.BlockSpec(memory_space=pltpu.SMEM),
            pl.BlockSpec((DIL_TQ, LANES), lambda h, i: (i, h)),
            _resident((s, LANES), lambda h, i: (0, pairs + h)),
            _resident((s, LANES), lambda h, i: (0, 2 * pairs + h)),
        ],
        out_specs=pl.BlockSpec((DIL_TQ, LANES), lambda h, i: (i, h)),
        out_shape=jax.ShapeDtypeStruct((s, A_HEADS * HEAD_DIM), BF16),
        scratch_shapes=[pltpu.VMEM((2, 2 * DIL_DMAX + 1, DIL_TQ, DIL_BT), F32)],
        compiler_params=pltpu.CompilerParams(
            dimension_semantics=("arbitrary", "arbitrary"), vmem_limit_bytes=VMEM_LIMIT),
        name="dilated",
    )(t5_table, proj, proj, proj)


NA_QROWS = 8
NA_KROWS = 16
NA_TQ = NA_QROWS * GRID_W
NA_TK = NA_KROWS * GRID_W
NA_RO = 2 * NA_ROWS - 1
NA_CO = 2 * NA_COLS - 1


def _natten_kernel(rpb_ref, q_ref, k_ref, v_ref, o_ref, tile_ref, bias_ref, *, rows):
    hp = pl.program_id(0)
    qi = pl.program_id(1)
    nq = pl.num_programs(1)
    rb = qi * NA_QROWS
    kb0 = jnp.clip(rb - NA_ROWS // 2, 0, rows - NA_KROWS)

    @pl.when(qi == 0)
    def _():
        c = lax.broadcasted_iota(jnp.int32, (GRID_W, LANES), 0)
        kc = lax.broadcasted_iota(jnp.int32, (GRID_W, LANES), 1) & (GRID_W - 1)
        co = jnp.clip(kc - c + (NA_COLS - 1), 0, NA_CO - 1)
        c0 = jnp.clip(c - NA_COLS // 2, 0, GRID_W - NA_COLS)
        col_ok = (kc >= c0) & (kc < c0 + NA_COLS)
        for hh in range(2):
            head = 2 * hp + hh

            def body(ro, carry):
                base = (head * NA_RO + ro) * NA_CO
                v = jnp.full(co.shape, rpb_ref[base + NA_CO - 1], F32)
                for k in range(NA_CO - 2, -1, -1):
                    v = jnp.where(co == k, rpb_ref[base + k], v)
                tile_ref[hh, ro] = jnp.where(col_ok, v * LOG2E, NEG)
                return carry

            lax.fori_loop(0, NA_RO, body, 0)

    @pl.when((qi == 0) | (qi == 1) | (qi == nq - 1))
    def _():
        lane = lax.broadcasted_iota(jnp.int32, (GRID_W, LANES), 1)
        left = lane < GRID_W
        for i in range(NA_QROWS):
            r = rb + i
            r0 = jnp.clip(r - NA_ROWS // 2, 0, rows - NA_ROWS)
            for a in range(NA_KROWS // 2):
                kr = kb0 + 2 * a
                ro = kr - r + (NA_ROWS - 1)
                ok0 = (kr >= r0) & (kr < r0 + NA_ROWS)
                ok1 = (kr + 1 >= r0) & (kr + 1 < r0 + NA_ROWS)
                ro0 = jnp.clip(ro, 0, NA_RO - 1)
                ro1 = jnp.clip(ro + 1, 0, NA_RO - 1)
                for hh in range(2):
                    t0 = jnp.where(ok0, tile_ref[hh, ro0], NEG)
                    t1 = jnp.where(ok1, tile_ref[hh, ro1], NEG)
                    bias_ref[hh, i * GRID_W:(i + 1) * GRID_W, a * LANES:(a + 1) * LANES] = (
                        jnp.where(left, t0, t1))

    kstart = pl.multiple_of(kb0 * GRID_W, GRID_W * 4)
    kw = k_ref[pl.ds(kstart, NA_TK), :]
    vw = v_ref[pl.ds(kstart, NA_TK), :]
    q = q_ref[...]
    lane = lax.broadcasted_iota(jnp.int32, q.shape, 1)
    outs = []
    for hh in range(2):
        in_head = (lane >= hh * HEAD_DIM) & (lane < (hh + 1) * HEAD_DIM)
        qm = jnp.where(in_head, q, jnp.zeros_like(q))
        s = _nt_dot(qm, kw) + bias_ref[hh]
        m = jnp.max(s, axis=-1, keepdims=True)
        p = jnp.exp2(s - m)
        l = jnp.sum(p, axis=-1, keepdims=True)
        o = jnp.dot(p.astype(BF16), vw, preferred_element_type=F32)
        outs.append(o / l)
    o_ref[...] = jnp.where(lane < HEAD_DIM, outs[0], outs[1]).astype(o_ref.dtype)


def _natten(proj, rpb_flat, col0):
    s = proj.shape[0]
    rows = s // GRID_W
    pairs = B_HEADS // 2
    return pl.pallas_call(
        functools.partial(_natten_kernel, rows=rows),
        grid=(pairs, s // NA_TQ),
        in_specs=[
            pl.BlockSpec(memory_space=pltpu.SMEM),
            pl.BlockSpec((NA_TQ, LANES), lambda h, i: (i, col0 + h)),
            _resident((s, LANES), lambda h, i: (0, col0 + pairs + h)),
            _resident((s, LANES), lambda h, i: (0, col0 + 2 * pairs + h)),
        ],
        out_specs=pl.BlockSpec((NA_TQ, LANES), lambda h, i: (i, h)),
        out_shape=jax.ShapeDtypeStruct((s, B_HEADS * HEAD_DIM), BF16),
        scratch_shapes=[
            pltpu.VMEM((2, NA_RO, GRID_W, LANES), F32),
            pltpu.VMEM((2, NA_TQ, NA_TK), F32),
        ],
        compiler_params=pltpu.CompilerParams(
            dimension_semantics=("arbitrary", "arbitrary"), vmem_limit_bytes=VMEM_LIMIT),
        name="natten",
    )(rpb_flat, proj, proj, proj)


FF_CHUNK = 256


def _mix_ffn_kernel(*refs, n_o, final):
    x_ref = refs[0]
    o_refs = refs[1:1 + n_o]
    wo_refs = refs[1 + n_o:1 + 2 * n_o]
    g_ref, wg_ref, wu_ref, wd_ref = refs[1 + 2 * n_o:5 + 2 * n_o]
    rest = refs[5 + 2 * n_o:]
    gf_ref = rest[0] if final else None
    out_ref = rest[-1]

    y = x_ref[...]
    for o_ref, wo_ref in zip(o_refs, wo_refs):
        y = y + jnp.dot(o_ref[...], wo_ref[...], preferred_element_type=F32)
    h = _rms(y, g_ref[...]).astype(BF16)
    d_ff = wg_ref.shape[1]
    acc = jnp.zeros(y.shape, F32)
    for c in range(0, d_ff, FF_CHUNK):
        gate = jnp.dot(h, wg_ref[:, c:c + FF_CHUNK], preferred_element_type=F32)
        up = jnp.dot(h, wu_ref[:, c:c + FF_CHUNK], preferred_element_type=F32)
        act = (gate * jax.nn.sigmoid(gate) * up).astype(BF16)
        acc = acc + jnp.dot(act, wd_ref[c:c + FF_CHUNK, :], preferred_element_type=F32)
    y = y + acc
    if final:
        y = _rms(y, gf_ref[...])
    out_ref[...] = y


def _mix_ffn(x, o_parts, wo_parts, g, wg, wu, wd, gf, tm):
    s, d = x.shape
    d_ff = wg.shape[1]
    assert d_ff % FF_CHUNK == 0
    n_o = len(o_parts)
    final = gf is not None
    in_specs = [pl.BlockSpec((tm, d), lambda i: (i, 0))]
    in_specs += [pl.BlockSpec((tm, o.shape[1]), lambda i: (i, 0)) for o in o_parts]
    in_specs += [_resident(w.shape, lambda i: (0, 0)) for w in wo_parts]
    in_specs += [
        _resident((1, d), lambda i: (0, 0)),
        _resident((d, d_ff), lambda i: (0, 0)),
        _resident((d, d_ff), lambda i: (0, 0)),
        _resident((d_ff, d), lambda i: (0, 0)),
    ]
    args = [x, *o_parts, *wo_parts, g.reshape(1, d), wg, wu, wd]
    if final:
        in_specs.append(_resident((1, d), lambda i: (0, 0)))
        args.append(gf.reshape(1, d))
    return pl.pallas_call(
        functools.partial(_mix_ffn_kernel, n_o=n_o, final=final),
        grid=(s // tm,),
        in_specs=in_specs,
        out_specs=pl.BlockSpec((tm, d), lambda i: (i, 0)),
        out_shape=jax.ShapeDtypeStruct((s, d), F32),
        compiler_params=pltpu.CompilerParams(
            dimension_semantics=("parallel",), vmem_limit_bytes=VMEM_LIMIT),
        name="mix_ffn",
    )(*args)


ROPE_LANE0 = QK_NOPE
ROPE_HALF = QK_ROPE // 2


def _rope(x, cos_t, sin_lo, sin_hi):
    return (x * cos_t + pltpu.roll(x, LANES - ROPE_HALF, 1) * sin_lo
            + pltpu.roll(x, ROPE_HALF, 1) * sin_hi)


def _mla_prep_kernel(x_ref, g_ref, win_ref, qn_ref, kvn_ref, wuq_ref, wuk_ref, wuv_ref,
                     cos_ref, slo_ref, shi_ref, q_out, k_out, v_out):
    h = _rms(x_ref[...], g_ref[...]).astype(BF16)
    proj = jnp.dot(h, win_ref[...], preferred_element_type=F32)
    cq = _rms(proj[:, :Q_LORA], qn_ref[...]).astype(BF16)
    ckv = _rms(proj[:, Q_LORA:Q_LORA + KV_LORA], kvn_ref[...]).astype(BF16)
    cos_t, sin_lo, sin_hi = cos_ref[...], slo_ref[...], shi_ref[...]
    kr = _rope(proj[:, Q_LORA + KV_LORA:], cos_t, sin_lo, sin_hi)
    q = jnp.dot(cq, wuq_ref[...], preferred_element_type=F32)
    kn = jnp.dot(ckv, wuk_ref[...], preferred_element_type=F32)
    v = jnp.dot(ckv, wuv_ref[...], preferred_element_type=F32)
    lane = lax.broadcasted_iota(jnp.int32, (1, LANES), 1)
    one_col = jnp.where(lane == V_DIM, 1.0, 0.0)
    for hd in range(C_HEADS):
        sl = slice(hd * LANES, (hd + 1) * LANES)
        q_out[hd] = _rope(q[:, sl], cos_t, sin_lo, sin_hi).astype(BF16)
        k_out[hd] = (kn[:, sl] + kr).astype(BF16)
        v_out[hd] = (v[:, sl] + one_col).astype(BF16)


def _mla_prep(x, g, win, qn, kvn, wuq, wuk, wuv, cos_t, sin_lo, sin_hi, tm):
    s, d = x.shape
    const = lambda i: (0, 0)
    row = lambda i: (i, 0)
    head_out = jax.ShapeDtypeStruct((C_HEADS, s, LANES), BF16)
    head_spec = pl.BlockSpec((C_HEADS, tm, LANES), lambda i: (0, i, 0))
    return pl.pallas_call(
        _mla_prep_kernel,
        grid=(s // tm,),
        in_specs=[
            pl.BlockSpec((tm, d), row),
            _resident((1, d), const),
            _resident(win.shape, const),
            _resident((1, Q_LORA), const),
            _resident((1, KV_LORA), const),
            _resident(wuq.shape, const),
            _resident(wuk.shape, const),
            _resident(wuv.shape, const),
            pl.BlockSpec((tm, LANES), row),
            pl.BlockSpec((tm, LANES), row),
            pl.BlockSpec((tm, LANES), row),
        ],
        out_specs=[head_spec, head_spec, head_spec],
        out_shape=[head_out, head_out, head_out],
        compiler_params=pltpu.CompilerParams(
            dimension_semantics=("parallel",), vmem_limit_bytes=VMEM_LIMIT),
        name="mla_prep",
    )(x, g.reshape(1, d), win, qn.reshape(1, -1), kvn.reshape(1, -1), wuq, wuk, wuv,
      cos_t, sin_lo, sin_hi)


MLA_TQ = 512
MLA_TK = 512
MLA_SLAB = 1024
MLA_ROWS = 32
MLA_BAND = (T5_FAR - 1 + MLA_TK - 1) // MLA_TK
MLA_FAR = MLA_BAND + 1


def _mla_attn_kernel(t5_ref, q_ref, qn_ref, k_ref, v_ref, o_ref,
                     bias_ref, s_ref, p_ref, m_ref, acc_ref, *, seq):
    hp = pl.program_id(0)
    qi = pl.program_id(1)
    n_slab = seq // MLA_SLAB
    tiles = MLA_SLAB // MLA_TK
    half = T5_BUCKETS // 2
    far_neg = [t5_ref[half - 1, 2 * hp + hh] * LOG2E for hh in range(2)]
    far_pos = [t5_ref[2 * half - 1, 2 * hp + hh] * LOG2E for hh in range(2)]

    @pl.when(qi == 0)
    def _():
        row = lax.broadcasted_iota(jnp.int32, (MLA_TQ, MLA_TK), 0)
        col = lax.broadcasted_iota(jnp.int32, (MLA_TQ, MLA_TK), 1)
        for hh in range(2):
            bias_ref[hh, 0] = jnp.full((MLA_TQ, MLA_TK), far_neg[hh], F32)
            bias_ref[hh, 2 * MLA_FAR] = jnp.full((MLA_TQ, MLA_TK), far_pos[hh], F32)
            for d in range(-MLA_BAND, MLA_BAND + 1):
                bias_ref[hh, d + MLA_FAR] = _t5_bias_tile(
                    t5_ref, 2 * hp + hh, col - row + d * MLA_TK)

    m_ref[...] = jnp.full(m_ref.shape, NEG, F32)
    acc_ref[...] = jnp.zeros(acc_ref.shape, F32)

    def scores(slab, slot, q_src):
        kstart = pl.multiple_of(slab * MLA_SLAB, MLA_SLAB)
        for hh in range(2):
            s_ref[slot, hh] = _nt_dot(q_src[hh], k_ref[hh, pl.ds(kstart, MLA_SLAB), :])

    def softmax_pv(slab, slot, banded):
        kstart = pl.multiple_of(slab * MLA_SLAB, MLA_SLAB)
        for hh in range(2):
            if banded:
                idx = [jnp.clip(slab * tiles + u - qi, -MLA_FAR, MLA_FAR) + MLA_FAR
                       for u in range(tiles)]
            else:
                c = jnp.where(slab * tiles < qi, far_neg[hh], far_pos[hh])
            alphas = []
            for r0 in range(0, MLA_TQ, MLA_ROWS):
                rows = slice(r0, r0 + MLA_ROWS)
                s = s_ref[slot, hh, rows, :]
                if banded:
                    s = s + jnp.concatenate(
                        [bias_ref[hh, idx[u], rows, :] for u in range(tiles)], axis=1)
                    m_blk = jnp.max(s, axis=-1, keepdims=True)
                else:
                    m_blk = jnp.max(s, axis=-1, keepdims=True) + c
                m_old = m_ref[hh, rows, :]
                m_new = jnp.maximum(m_old, m_blk)
                m_ref[hh, rows, :] = m_new
                alphas.append(jnp.exp2(m_old - m_new))
                shift = m_new if banded else m_new - c
                p_ref[hh, rows, :] = jnp.exp2(s - shift).astype(BF16)
            alpha = jnp.concatenate(alphas, axis=0)
            acc_ref[hh] = alpha * acc_ref[hh] + jnp.dot(
                p_ref[hh], v_ref[hh, pl.ds(kstart, MLA_SLAB), :], preferred_element_type=F32)

    def work(slab, slot, nxt, banded):
        scores(nxt[0], 1 - slot, nxt[1])
        softmax_pv(slab, slot, banded)

    def is_banded(slab):
        d0 = slab * tiles - qi
        return (d0 >= -MLA_BAND - (tiles - 1)) & (d0 <= MLA_BAND)

    def stage(slab, slot, nxt):
        banded = is_banded(slab)
        pl.when(banded)(functools.partial(work, slab, slot, nxt, True))
        pl.when(jnp.logical_not(banded))(functools.partial(work, slab, slot, nxt, False))

    def stage_pair(slab, nxt):
        any_banded = is_banded(slab) | is_banded(slab + 1)

        @pl.when(jnp.logical_not(any_banded))
        def _():
            work(slab, 0, (slab + 1, q_ref), False)
            work(slab + 1, 1, nxt, False)

        @pl.when(any_banded)
        def _():
            stage(slab, 0, (slab + 1, q_ref))
            stage(slab + 1, 1, nxt)

    pl.when(qi == 0)(functools.partial(scores, 0, 0, q_ref))

    def loop_body(t, carry):
        stage_pair(2 * t, (2 * t + 2, q_ref))
        return carry

    lax.fori_loop(0, n_slab // 2 - 1, loop_body, 0)
    stage_pair(n_slab - 2, (0, qn_ref))

    lane = lax.broadcasted_iota(jnp.int32, (MLA_TQ, LANES), 1)
    outs = [acc_ref[hh] / acc_ref[hh, :, V_DIM:V_DIM + 1] for hh in range(2)]
    out = jnp.where(lane < V_DIM, outs[0], pltpu.roll(outs[1], V_DIM, 1))
    o_ref[...] = out.astype(o_ref.dtype)


def _mla_attn(q, k, v, t5_table):
    heads, s, _ = q.shape
    pairs = heads // 2
    assert MLA_TQ == MLA_TK and s % (2 * MLA_SLAB) == 0 and MLA_SLAB % MLA_TK == 0
    nq = s // MLA_TQ
    return pl.pallas_call(
        functools.partial(_mla_attn_kernel, seq=s),
        grid=(pairs, nq),
        in_specs=[
            pl.BlockSpec(memory_space=pltpu.SMEM),
            pl.BlockSpec((2, MLA_TQ, LANES), lambda h, i: (h, i, 0)),
            pl.BlockSpec((2, MLA_TQ, LANES), lambda h, i: (h, jnp.minimum(i + 1, nq - 1), 0)),
            _resident((2, s, LANES), lambda h, i: (h, 0, 0)),
            _resident((2, s, LANES), lambda h, i: (h, 0, 0)),
        ],
        out_specs=pl.BlockSpec((MLA_TQ, LANES), lambda h, i: (i, h)),
        out_shape=jax.ShapeDtypeStruct((s, heads * V_DIM), BF16),
        scratch_shapes=[
            pltpu.VMEM((2, 2 * MLA_FAR + 1, MLA_TQ, MLA_TK), F32),
            pltpu.VMEM((2, 2, MLA_TQ, MLA_SLAB), F32),
            pltpu.VMEM((2, MLA_TQ, MLA_SLAB), BF16),
            pltpu.VMEM((2, MLA_TQ, 1), F32),
            pltpu.VMEM((2, MLA_TQ, LANES), F32),
        ],
        compiler_params=pltpu.CompilerParams(
            dimension_semantics=("arbitrary", "arbitrary"), vmem_limit_bytes=VMEM_LIMIT),
        name="mla_attn",
    )(t5_table, q, q, k, v)


def _per_head_lanes(w, heads, width, lane0=0):
    k = w.shape[0]
    w = w.reshape(k, heads, width)
    w = jnp.pad(w, ((0, 0), (0, 0), (lane0, LANES - lane0 - width)))
    return w.reshape(k, heads * LANES)


def _rope_tables(s):
    pos = jnp.arange(s, dtype=F32)
    freqs = ROPE_THETA ** (-jnp.arange(0, QK_ROPE, 2, dtype=F32) / QK_ROPE)
    ang = pos[:, None] * freqs[None, :]
    cos, sin = jnp.cos(ang), jnp.sin(ang)
    zeros = jnp.zeros((s, ROPE_HALF), F32)
    tail = LANES - ROPE_LANE0 - QK_ROPE
    cos_t = jnp.concatenate([jnp.ones((s, ROPE_LANE0), F32), cos, cos, jnp.ones((s, tail), F32)], axis=1)
    sin_lo = jnp.concatenate([jnp.zeros((s, ROPE_LANE0), F32), -sin, zeros, jnp.zeros((s, tail), F32)], axis=1)
    sin_hi = jnp.concatenate([jnp.zeros((s, ROPE_LANE0), F32), zeros, sin, jnp.zeros((s, tail), F32)], axis=1)
    return cos_t, sin_lo, sin_hi


def kernel(x, t5_table, attn_norm, ffn_norm, even_w_in, na_rpb, even_w_out, odd_w_in, mla_q_norm,
           mla_w_uq, mla_kv_norm, mla_w_uk, mla_w_uv, odd_w_out, ffn_w_gate, ffn_w_up, ffn_w_down,
           final_norm):
    b, s, d = x.shape
    assert b == 1 and s % (GRID_W * NA_KROWS) == 0 and s % MLA_TQ == 0
    x0 = x.reshape(s, d)
    tm = 512

    a_cols = A_HEADS * HEAD_DIM
    b_cols = B_HEADS * HEAD_DIM
    qscale = HEAD_DIM ** -0.5 * LOG2E
    w_in = even_w_in[0]
    col_scale = jnp.concatenate([
        jnp.full((a_cols,), qscale, F32), jnp.ones((2 * a_cols,), F32),
        jnp.full((b_cols,), qscale, F32), jnp.ones((2 * b_cols,), F32)])
    proj = _norm_proj(x0, attn_norm[0], (w_in * col_scale).astype(BF16), tm)
    oa = _dilated(proj, t5_table)
    ob = _natten(proj, na_rpb[0].reshape(-1), 3 * a_cols // LANES)
    w_out = even_w_out[0].astype(BF16)
    x1 = _mix_ffn(x0, [oa, ob], [w_out[:a_cols], w_out[a_cols:]], ffn_norm[0],
                  ffn_w_gate[0].astype(BF16), ffn_w_up[0].astype(BF16), ffn_w_down[0].astype(BF16),
                  None, tm)

    cscale = (QK_NOPE + QK_ROPE) ** -0.5 * LOG2E
    w_in1 = odd_w_in[0]
    win = jnp.concatenate([
        w_in1[:, :Q_LORA + KV_LORA],
        _per_head_lanes(w_in1[:, Q_LORA + KV_LORA:], 1, QK_ROPE, ROPE_LANE0)], axis=1).astype(BF16)
    wuq = (_per_head_lanes(mla_w_uq[0], C_HEADS, QK_NOPE + QK_ROPE) * cscale).astype(BF16)
    wuk = _per_head_lanes(mla_w_uk[0], C_HEADS, QK_NOPE).astype(BF16)
    wuv = _per_head_lanes(mla_w_uv[0], C_HEADS, V_DIM).astype(BF16)
    cos_t, sin_lo, sin_hi = _rope_tables(s)
    q1, k1, v1 = _mla_prep(x1, attn_norm[1], win, mla_q_norm[0], mla_kv_norm[0], wuq, wuk, wuv,
                           cos_t, sin_lo, sin_hi, tm)
    oc = _mla_attn(q1, k1, v1, t5_table)
    out = _mix_ffn(x1, [oc], [odd_w_out[0].astype(BF16)], ffn_norm[1],
                   ffn_w_gate[1].astype(BF16), ffn_w_up[1].astype(BF16), ffn_w_down[1].astype(BF16),
                   final_norm, tm)
    return out.reshape(b, s, d)
```

```python
import functools
import math

import numpy as np
import jax
import jax.numpy as jnp
from jax import lax
from jax.experimental import pallas as pl
from jax.experimental.pallas import tpu as pltpu

F32 = jnp.float32
BF16 = jnp.bfloat16

HEAD_DIM = 64
GRID_W = 64
RMS_EPS = 1e-6
A_HEADS = 8
B_HEADS = 8
DILATED_BRANCHES = ((128, 1), (512, 4), (2048, 16))
NA_ROWS = 8
NA_COLS = 16
C_HEADS = 16
Q_LORA = 384
KV_LORA = 128
QK_NOPE = 64
QK_ROPE = 32
V_DIM = 64
ROPE_THETA = 10000.0
T5_BUCKETS = 32
T5_MAX_DIST = 1024

LOG2E = math.log2(math.e)
NEG = -1e30
LANES = 128
VMEM_LIMIT = 62 * 1024 * 1024


def _t5_upper_bounds():
    half = T5_BUCKETS // 2
    max_exact = half // 2
    n = np.arange(0, 4 * T5_MAX_DIST, dtype=np.int64)
    nf = np.maximum(n, max_exact).astype(np.float32)
    val = (np.log(nf / np.float32(max_exact)) / np.float32(math.log(T5_MAX_DIST / max_exact))
           * np.float32(half - max_exact))
    large = np.minimum(max_exact + val.astype(np.int32), half - 1)
    bucket = np.where(n < max_exact, n, large)
    return tuple(int(np.argmax(bucket > b)) for b in range(half - 1))


T5_UPPER = _t5_upper_bounds()
T5_FAR = T5_UPPER[-1]


def _t5_bias_tile(t5_ref, head, rel):
    half = T5_BUCKETS // 2
    n = jnp.abs(rel)
    vneg = jnp.full(rel.shape, t5_ref[half - 1, head], F32)
    vpos = jnp.full(rel.shape, t5_ref[2 * half - 1, head], F32)
    for b in range(half - 2, -1, -1):
        inb = n < T5_UPPER[b]
        vneg = jnp.where(inb, t5_ref[b, head], vneg)
        vpos = jnp.where(inb, t5_ref[half + b, head], vpos)
    return jnp.where(rel > 0, vpos, vneg) * LOG2E


def _rms(x, g):
    return x * lax.rsqrt(jnp.mean(x * x, axis=-1, keepdims=True) + RMS_EPS) * g


def _nt_dot(a, b):
    return lax.dot_general(a, b, (((1,), (1,)), ((), ())), preferred_element_type=F32)


def _resident(shape, index_map):
    return pl.BlockSpec(shape, index_map, pipeline_mode=pl.Buffered(1))


def _norm_proj_kernel(x_ref, g_ref, w_ref, o_ref):
    h = _rms(x_ref[...], g_ref[...]).astype(BF16)
    o_ref[...] = jnp.dot(h, w_ref[...], preferred_element_type=F32).astype(o_ref.dtype)


def _norm_proj(x, g, w, tm):
    s, d = x.shape
    n = w.shape[1]
    return pl.pallas_call(
        _norm_proj_kernel,
        grid=(s // tm,),
        in_specs=[
            pl.BlockSpec((tm, d), lambda i: (i, 0)),
            _resident((1, d), lambda i: (0, 0)),
            _resident((d, n), lambda i: (0, 0)),
        ],
        out_specs=pl.BlockSpec((tm, n), lambda i: (i, 0)),
        out_shape=jax.ShapeDtypeStruct((s, n), BF16),
        compiler_params=pltpu.CompilerParams(
            dimension_semantics=("parallel",), vmem_limit_bytes=VMEM_LIMIT),
        name="norm_proj",
    )(x, g.reshape(1, d), w)


DIL_REACH = max(w // 2 for w, _ in DILATED_BRANCHES)
DIL_TQ = 256
DIL_W = DIL_TQ + 2 * DIL_REACH
DIL_BT = 256
DIL_WT = DIL_W // DIL_BT
DIL_DMAX = -(-(DIL_REACH + 1) // DIL_BT)


def _dilated_kernel(t5_ref, q_ref, k_ref, v_ref, o_ref, bias_ref, *, seq):
    hp = pl.program_id(0)
    i = pl.program_id(1)

    @pl.when(i == 0)
    def _():
        row = lax.broadcasted_iota(jnp.int32, (DIL_TQ, DIL_BT), 0)
        col = lax.broadcasted_iota(jnp.int32, (DIL_TQ, DIL_BT), 1)
        for d in range(-DIL_DMAX, DIL_DMAX + 1):
            rel = col - row + d * DIL_BT
            n = jnp.abs(rel)
            count = jnp.zeros(rel.shape, jnp.int32)
            for window, dil in DILATED_BRANCHES:
                member = ((rel & (dil - 1)) == 0) & (n <= (window // 2))
                count = count + member.astype(jnp.int32)
            logmult = jnp.where(count == 3, math.log2(3.0), jnp.where(count == 2, 1.0, 0.0))
            for hh in range(2):
                b = _t5_bias_tile(t5_ref, 2 * hp + hh, rel) + logmult
                bias_ref[hh, d + DIL_DMAX] = jnp.where(count > 0, b, NEG)

    t0 = i * DIL_TQ
    start = pl.multiple_of(jnp.clip(t0 - DIL_REACH, 0, seq - DIL_W), DIL_BT)
    d0 = (start - t0) // DIL_BT
    kw = k_ref[pl.ds(start, DIL_W), :]
    vw = v_ref[pl.ds(start, DIL_W), :]
    q = q_ref[...]
    lane = lax.broadcasted_iota(jnp.int32, q.shape, 1)

    outs = []
    for hh in range(2):
        in_head = (lane >= hh * HEAD_DIM) & (lane < (hh + 1) * HEAD_DIM)
        qm = jnp.where(in_head, q, jnp.zeros_like(q))
        bias = jnp.concatenate(
            [bias_ref[hh, jnp.clip(d0 + k, -DIL_DMAX, DIL_DMAX) + DIL_DMAX]
             for k in range(DIL_WT)], axis=1)
        s = _nt_dot(qm, kw) + bias
        m = jnp.max(s, axis=-1, keepdims=True)
        p = jnp.exp2(s - m)
        l = jnp.sum(p, axis=-1, keepdims=True)
        o = jnp.dot(p.astype(BF16), vw, preferred_element_type=F32)
        outs.append(o / l)
    o_ref[...] = jnp.where(lane < HEAD_DIM, outs[0], outs[1]).astype(o_ref.dtype)


def _dilated(proj, t5_table):
    s = proj.shape[0]
    pairs = A_HEADS // 2
    assert s % DIL_BT == 0 and s >= DIL_W and DIL_W % DIL_BT == 0 and DIL_REACH % DIL_BT == 0
    return pl.pallas_call(
        functools.partial(_dilated_kernel, seq=s),
        grid=(pairs, s // DIL_TQ),
        in_specs=[
            pl.BlockSpec(memory_space=pltpu.SMEM),
            pl.BlockSpec((DIL_TQ, LANES), lambda h, i: (i, h)),
            _resident((s, LANES), lambda h, i: (0, pairs + h)),
            _resident((s, LANES), lambda h, i: (0, 2 * pairs + h)),
        ],
        out_specs=pl.BlockSpec((DIL_TQ, LANES), lambda h, i: (i, h)),
        out_shape=jax.ShapeDtypeStruct((s, A_HEADS * HEAD_DIM), BF16),
        scratch_shapes=[pltpu.VMEM((2, 2 * DIL_DMAX + 1, DIL_TQ, DIL_BT), F32)],
        compiler_params=pltpu.CompilerParams(
            dimension_semantics=("arbitrary", "arbitrary"), vmem_limit_bytes=VMEM_LIMIT),
        name="dilated",
    )(t5_table, proj, proj, proj)


NA_QROWS = 8
NA_KROWS = 16
NA_TQ = NA_QROWS * GRID_W
NA_TK = NA_KROWS * GRID_W
NA_RO = 2 * NA_ROWS - 1
NA_CO = 2 * NA_COLS - 1


def _natten_kernel(rpb_ref, q_ref, k_ref, v_ref, o_ref, tile_ref, bias_ref, *, rows):
    hp = pl.program_id(0)
    qi = pl.program_id(1)
    nq = pl.num_programs(1)
    rb = qi * NA_QROWS
    kb0 = jnp.clip(rb - NA_ROWS // 2, 0, rows - NA_KROWS)

    @pl.when(qi == 0)
    def _():
        c = lax.broadcasted_iota(jnp.int32, (GRID_W, LANES), 0)
        kc = lax.broadcasted_iota(jnp.int32, (GRID_W, LANES), 1) & (GRID_W - 1)
        co = jnp.clip(kc - c + (NA_COLS - 1), 0, NA_CO - 1)
        c0 = jnp.clip(c - NA_COLS // 2, 0, GRID_W - NA_COLS)
        col_ok = (kc >= c0) & (kc < c0 + NA_COLS)
        for hh in range(2):
            head = 2 * hp + hh

            def body(ro, carry):
                base = (head * NA_RO + ro) * NA_CO
                v = jnp.full(co.shape, rpb_ref[base + NA_CO - 1], F32)
                for k in range(NA_CO - 2, -1, -1):
                    v = jnp.where(co == k, rpb_ref[base + k], v)
                tile_ref[hh, ro] = jnp.where(col_ok, v * LOG2E, NEG)
                return carry

            lax.fori_loop(0, NA_RO, body, 0)

    @pl.when((qi == 0) | (qi == 1) | (qi == nq - 1))
    def _():
        lane = lax.broadcasted_iota(jnp.int32, (GRID_W, LANES), 1)
        left = lane < GRID_W
        for i in range(NA_QROWS):
            r = rb + i
            r0 = jnp.clip(r - NA_ROWS // 2, 0, rows - NA_ROWS)
            for a in range(NA_KROWS // 2):
                kr = kb0 + 2 * a
                ro = kr - r + (NA_ROWS - 1)
                ok0 = (kr >= r0) & (kr < r0 + NA_ROWS)
                ok1 = (kr + 1 >= r0) & (kr + 1 < r0 + NA_ROWS)
                ro0 = jnp.clip(ro, 0, NA_RO - 1)
                ro1 = jnp.clip(ro + 1, 0, NA_RO - 1)
                for hh in range(2):
                    t0 = jnp.where(ok0, tile_ref[hh, ro0], NEG)
                    t1 = jnp.where(ok1, tile_ref[hh, ro1], NEG)
                    bias_ref[hh, i * GRID_W:(i + 1) * GRID_W, a * LANES:(a + 1) * LANES] = (
                        jnp.where(left, t0, t1))

    kstart = pl.multiple_of(kb0 * GRID_W, GRID_W * 4)
    kw = k_ref[pl.ds(kstart, NA_TK), :]
    vw = v_ref[pl.ds(kstart, NA_TK), :]
    q = q_ref[...]
    lane = lax.broadcasted_iota(jnp.int32, q.shape, 1)
    outs = []
    for hh in range(2):
        in_head = (lane >= hh * HEAD_DIM) & (lane < (hh + 1) * HEAD_DIM)
        qm = jnp.where(in_head, q, jnp.zeros_like(q))
        s = _nt_dot(qm, kw) + bias_ref[hh]
        m = jnp.max(s, axis=-1, keepdims=True)
        p = jnp.exp2(s - m)
        l = jnp.sum(p, axis=-1, keepdims=True)
        o = jnp.dot(p.astype(BF16), vw, preferred_element_type=F32)
        outs.append(o / l)
    o_ref[...] = jnp.where(lane < HEAD_DIM, outs[0], outs[1]).astype(o_ref.dtype)


def _natten(proj, rpb_flat, col0):
    s = proj.shape[0]
    rows = s // GRID_W
    pairs = B_HEADS // 2
    return pl.pallas_call(
        functools.partial(_natten_kernel, rows=rows),
        grid=(pairs, s // NA_TQ),
        in_specs=[
            pl.BlockSpec(memory_space=pltpu.SMEM),
            pl.BlockSpec((NA_TQ, LANES), lambda h, i: (i, col0 + h)),
            _resident((s, LANES), lambda h, i: (0, col0 + pairs + h)),
            _resident((s, LANES), lambda h, i: (0, col0 + 2 * pairs + h)),
        ],
        out_specs=pl.BlockSpec((NA_TQ, LANES), lambda h, i: (i, h)),
        out_shape=jax.ShapeDtypeStruct((s, B_HEADS * HEAD_DIM), BF16),
        scratch_shapes=[
            pltpu.VMEM((2, NA_RO, GRID_W, LANES), F32),
            pltpu.VMEM((2, NA_TQ, NA_TK), F32),
        ],
        compiler_params=pltpu.CompilerParams(
            dimension_semantics=("arbitrary", "arbitrary"), vmem_limit_bytes=VMEM_LIMIT),
        name="natten",
    )(rpb_flat, proj, proj, proj)


FF_CHUNK = 256


def _mix_ffn_kernel(*refs, n_o, final):
    x_ref = refs[0]
    o_refs = refs[1:1 + n_o]
    wo_refs = refs[1 + n_o:1 + 2 * n_o]
    g_ref, wg_ref, wu_ref, wd_ref = refs[1 + 2 * n_o:5 + 2 * n_o]
    rest = refs[5 + 2 * n_o:]
    gf_ref = rest[0] if final else None
    out_ref = rest[-1]

    y = x_ref[...]
    for o_ref, wo_ref in zip(o_refs, wo_refs):
        y = y + jnp.dot(o_ref[...], wo_ref[...], preferred_element_type=F32)
    h = _rms(y, g_ref[...]).astype(BF16)
    d_ff = wg_ref.shape[1]
    acc = jnp.zeros(y.shape, F32)
    for c in range(0, d_ff, FF_CHUNK):
        gate = jnp.dot(h, wg_ref[:, c:c + FF_CHUNK], preferred_element_type=F32)
        up = jnp.dot(h, wu_ref[:, c:c + FF_CHUNK], preferred_element_type=F32)
        act = (gate * jax.nn.sigmoid(gate) * up).astype(BF16)
        acc = acc + jnp.dot(act, wd_ref[c:c + FF_CHUNK, :], preferred_element_type=F32)
    y = y + acc
    if final:
        y = _rms(y, gf_ref[...])
    out_ref[...] = y


def _mix_ffn(x, o_parts, wo_parts, g, wg, wu, wd, gf, tm):
    s, d = x.shape
    d_ff = wg.shape[1]
    assert d_ff % FF_CHUNK == 0
    n_o = len(o_parts)
    final = gf is not None
    in_specs = [pl.BlockSpec((tm, d), lambda i: (i, 0))]
    in_specs += [pl.BlockSpec((tm, o.shape[1]), lambda i: (i, 0)) for o in o_parts]
    in_specs += [_resident(w.shape, lambda i: (0, 0)) for w in wo_parts]
    in_specs += [
        _resident((1, d), lambda i: (0, 0)),
        _resident((d, d_ff), lambda i: (0, 0)),
        _resident((d, d_ff), lambda i: (0, 0)),
        _resident((d_ff, d), lambda i: (0, 0)),
    ]
    args = [x, *o_parts, *wo_parts, g.reshape(1, d), wg, wu, wd]
    if final:
        in_specs.append(_resident((1, d), lambda i: (0, 0)))
        args.append(gf.reshape(1, d))
    return pl.pallas_call(
        functools.partial(_mix_ffn_kernel, n_o=n_o, final=final),
        grid=(s // tm,),
        in_specs=in_specs,
        out_specs=pl.BlockSpec((tm, d), lambda i: (i, 0)),
        out_shape=jax.ShapeDtypeStruct((s, d), F32),
        compiler_params=pltpu.CompilerParams(
            dimension_semantics=("parallel",), vmem_limit_bytes=VMEM_LIMIT),
        name="mix_ffn",
    )(*args)


ROPE_LANE0 = QK_NOPE
ROPE_HALF = QK_ROPE // 2


def _rope(x, cos_t, sin_lo, sin_hi):
    return (x * cos_t + pltpu.roll(x, LANES - ROPE_HALF, 1) * sin_lo
            + pltpu.roll(x, ROPE_HALF, 1) * sin_hi)


def _mla_prep_kernel(x_ref, g_ref, win_ref, qn_ref, kvn_ref, wuq_ref, wuk_ref, wuv_ref,
                     cos_ref, slo_ref, shi_ref, q_out, k_out, v_out):
    h = _rms(x_ref[...], g_ref[...]).astype(BF16)
    proj = jnp.dot(h, win_ref[...], preferred_element_type=F32)
    cq = _rms(proj[:, :Q_LORA], qn_ref[...]).astype(BF16)
    ckv = _rms(proj[:, Q_LORA:Q_LORA + KV_LORA], kvn_ref[...]).astype(BF16)
    cos_t, sin_lo, sin_hi = cos_ref[...], slo_ref[...], shi_ref[...]
    kr = _rope(proj[:, Q_LORA + KV_LORA:], cos_t, sin_lo, sin_hi)
    q = jnp.dot(cq, wuq_ref[...], preferred_element_type=F32)
    kn = jnp.dot(ckv, wuk_ref[...], preferred_element_type=F32)
    v = jnp.dot(ckv, wuv_ref[...], preferred_element_type=F32)
    lane = lax.broadcasted_iota(jnp.int32, (1, LANES), 1)
    one_col = jnp.where(lane == V_DIM, 1.0, 0.0)
    for hd in range(C_HEADS):
        sl = slice(hd * LANES, (hd + 1) * LANES)
        q_out[hd] = _rope(q[:, sl], cos_t, sin_lo, sin_hi).astype(BF16)
        k_out[hd] = (kn[:, sl] + kr).astype(BF16)
        v_out[hd] = (v[:, sl] + one_col).astype(BF16)


def _mla_prep(x, g, win, qn, kvn, wuq, wuk, wuv, cos_t, sin_lo, sin_hi, tm):
    s, d = x.shape
    const = lambda i: (0, 0)
    row = lambda i: (i, 0)
    head_out = jax.ShapeDtypeStruct((C_HEADS, s, LANES), BF16)
    head_spec = pl.BlockSpec((C_HEADS, tm, LANES), lambda i: (0, i, 0))
    return pl.pallas_call(
        _mla_prep_kernel,
        grid=(s // tm,),
        in_specs=[
            pl.BlockSpec((tm, d), row),
            _resident((1, d), const),
            _resident(win.shape, const),
            _resident((1, Q_LORA), const),
            _resident((1, KV_LORA), const),
            _resident(wuq.shape, const),
            _resident(wuk.shape, const),
            _resident(wuv.shape, const),
            pl.BlockSpec((tm, LANES), row),
            pl.BlockSpec((tm, LANES), row),
            pl.BlockSpec((tm, LANES), row),
        ],
        out_specs=[head_spec, head_spec, head_spec],
        out_shape=[head_out, head_out, head_out],
        compiler_params=pltpu.CompilerParams(
            dimension_semantics=("parallel",), vmem_limit_bytes=VMEM_LIMIT),
        name="mla_prep",
    )(x, g.reshape(1, d), win, qn.reshape(1, -1), kvn.reshape(1, -1), wuq, wuk, wuv,
      cos_t, sin_lo, sin_hi)


MLA_TQ = 512
MLA_TK = 512
MLA_SLAB = 2048
MLA_ROWS = 16
MLA_BAND = (T5_FAR - 1 + MLA_TK - 1) // MLA_TK
MLA_FAR = MLA_BAND + 1


def _mla_attn_kernel(t5_ref, q_ref, qn_ref, k_ref, v_ref, o_ref,
                     bias_ref, s_ref, p_ref, m_ref, acc_ref, *, seq):
    hp = pl.program_id(0)
    qi = pl.program_id(1)
    n_slab = seq // MLA_SLAB
    tiles = MLA_SLAB // MLA_TK
    half = T5_BUCKETS // 2
    far_neg = [t5_ref[half - 1, 2 * hp + hh] * LOG2E for hh in range(2)]
    far_pos = [t5_ref[2 * half - 1, 2 * hp + hh] * LOG2E for hh in range(2)]

    @pl.when(qi == 0)
    def _():
        row = lax.broadcasted_iota(jnp.int32, (MLA_TQ, MLA_TK), 0)
        col = lax.broadcasted_iota(jnp.int32, (MLA_TQ, MLA_TK), 1)
        for hh in range(2):
            bias_ref[hh, 0] = jnp.full((MLA_TQ, MLA_TK), far_neg[hh], F32)
            bias_ref[hh, 2 * MLA_FAR] = jnp.full((MLA_TQ, MLA_TK), far_pos[hh], F32)
            for d in range(-MLA_BAND, MLA_BAND + 1):
                bias_ref[hh, d + MLA_FAR] = _t5_bias_tile(
                    t5_ref, 2 * hp + hh, col - row + d * MLA_TK)

    m_ref[...] = jnp.full(m_ref.shape, NEG, F32)
    acc_ref[...] = jnp.zeros(acc_ref.shape, F32)

    def scores(slab, slot, q_src):
        kstart = pl.multiple_of(slab * MLA_SLAB, MLA_SLAB)
        for hh in range(2):
            s_ref[slot, hh] = _nt_dot(q_src[hh], k_ref[hh, pl.ds(kstart, MLA_SLAB), :])

    def softmax_pv(slab, slot, banded):
        kstart = pl.multiple_of(slab * MLA_SLAB, MLA_SLAB)
        for hh in range(2):
            if banded:
                idx = [jnp.clip(slab * tiles + u - qi, -MLA_FAR, MLA_FAR) + MLA_FAR
                       for u in range(tiles)]
            else:
                c = jnp.where(slab * tiles < qi, far_neg[hh], far_pos[hh])
            alphas = []
            for r0 in range(0, MLA_TQ, MLA_ROWS):
                rows = slice(r0, r0 + MLA_ROWS)
                s = s_ref[slot, hh, rows, :]
                if banded:
                    s = s + jnp.concatenate(
                        [bias_ref[hh, idx[u], rows, :] for u in range(tiles)], axis=1)
                    m_blk = jnp.max(s, axis=-1, keepdims=True)
                else:
                    m_blk = jnp.max(s, axis=-1, keepdims=True) + c
                m_old = m_ref[hh, rows, :]
                m_new = jnp.maximum(m_old, m_blk)
                m_ref[hh, rows, :] = m_new
                alphas.append(jnp.exp2(m_old - m_new))
                shift = m_new if banded else m_new - c
                p_ref[hh, rows, :] = jnp.exp2(s - shift).astype(BF16)
            alpha = jnp.concatenate(alphas, axis=0)
            acc_ref[hh] = alpha * acc_ref[hh] + jnp.dot(
                p_ref[hh], v_ref[hh, pl.ds(kstart, MLA_SLAB), :], preferred_element_type=F32)

    def work(slab, slot, nxt, banded):
        scores(nxt[0], 1 - slot, nxt[1])
        softmax_pv(slab, slot, banded)

    def is_banded(slab):
        d0 = slab * tiles - qi
        return (d0 >= -MLA_BAND - (tiles - 1)) & (d0 <= MLA_BAND)

    def stage(slab, slot, nxt):
        banded = is_banded(slab)
        pl.when(banded)(functools.partial(work, slab, slot, nxt, True))
        pl.when(jnp.logical_not(banded))(functools.partial(work, slab, slot, nxt, False))

    def stage_pair(slab, nxt):
        stage(slab, 0, (slab + 1, q_ref))
        stage(slab + 1, 1, nxt)

    pl.when(qi == 0)(functools.partial(scores, 0, 0, q_ref))

    def loop_body(t, carry):
        stage_pair(2 * t, (2 * t + 2, q_ref))
        return carry

    lax.fori_loop(0, n_slab // 2 - 1, loop_body, 0)
    stage_pair(n_slab - 2, (0, qn_ref))

    lane = lax.broadcasted_iota(jnp.int32, (MLA_TQ, LANES), 1)
    outs = [acc_ref[hh] / acc_ref[hh, :, V_DIM:V_DIM + 1] for hh in range(2)]
    out = jnp.where(lane < V_DIM, outs[0], pltpu.roll(outs[1], V_DIM, 1))
    o_ref[...] = out.astype(o_ref.dtype)


def _mla_attn(q, k, v, t5_table):
    heads, s, _ = q.shape
    pairs = heads // 2
    assert MLA_TQ == MLA_TK and s % (2 * MLA_SLAB) == 0 and MLA_SLAB % MLA_TK == 0
    nq = s // MLA_TQ
    return pl.pallas_call(
        functools.partial(_mla_attn_kernel, seq=s),
        grid=(pairs, nq),
        in_specs=[
            pl.BlockSpec(memory_space=pltpu.SMEM),
            pl.BlockSpec((2, MLA_TQ, LANES), lambda h, i: (h, i, 0)),
            pl.BlockSpec((2, MLA_TQ, LANES), lambda h, i: (h, jnp.minimum(i + 1, nq - 1), 0)),
            _resident((2, s, LANES), lambda h, i: (h, 0, 0)),
            _resident((2, s, LANES), lambda h, i: (h, 0, 0)),
        ],
        out_specs=pl.BlockSpec((MLA_TQ, LANES), lambda h, i: (i, h)),
        out_shape=jax.ShapeDtypeStruct((s, heads * V_DIM), BF16),
        scratch_shapes=[
            pltpu.VMEM((2, 2 * MLA_FAR + 1, MLA_TQ, MLA_TK), F32),
            pltpu.VMEM((2, 2, MLA_TQ, MLA_SLAB), F32),
            pltpu.VMEM((2, MLA_TQ, MLA_SLAB), BF16),
            pltpu.VMEM((2, MLA_TQ, 1), F32),
            pltpu.VMEM((2, MLA_TQ, LANES), F32),
        ],
        compiler_params=pltpu.CompilerParams(
            dimension_semantics=("arbitrary", "arbitrary"), vmem_limit_bytes=VMEM_LIMIT),
        name="mla_attn",
    )(t5_table, q, q, k, v)


def _per_head_lanes(w, heads, width, lane0=0):
    k = w.shape[0]
    w = w.reshape(k, heads, width)
    w = jnp.pad(w, ((0, 0), (0, 0), (lane0, LANES - lane0 - width)))
    return w.reshape(k, heads * LANES)


def _rope_tables(s):
    pos = jnp.arange(s, dtype=F32)
    freqs = ROPE_THETA ** (-jnp.arange(0, QK_ROPE, 2, dtype=F32) / QK_ROPE)
    ang = pos[:, None] * freqs[None, :]
    cos, sin = jnp.cos(ang), jnp.sin(ang)
    zeros = jnp.zeros((s, ROPE_HALF), F32)
    tail = LANES - ROPE_LANE0 - QK_ROPE
    cos_t = jnp.concatenate([jnp.ones((s, ROPE_LANE0), F32), cos, cos, jnp.ones((s, tail), F32)], axis=1)
    sin_lo = jnp.concatenate([jnp.zeros((s, ROPE_LANE0), F32), -sin, zeros, jnp.zeros((s, tail), F32)], axis=1)
    sin_hi = jnp.concatenate([jnp.zeros((s, ROPE_LANE0), F32), zeros, sin, jnp.zeros((s, tail), F32)], axis=1)
    return cos_t, sin_lo, sin_hi


def kernel(x, t5_table, attn_norm, ffn_norm, even_w_in, na_rpb, even_w_out, odd_w_in, mla_q_norm,
           mla_w_uq, mla_kv_norm, mla_w_uk, mla_w_uv, odd_w_out, ffn_w_gate, ffn_w_up, ffn_w_down,
           final_norm):
    b, s, d = x.shape
    assert b == 1 and s % (GRID_W * NA_KROWS) == 0 and s % MLA_TQ == 0
    x0 = x.reshape(s, d)
    tm = 512

    a_cols = A_HEADS * HEAD_DIM
    b_cols = B_HEADS * HEAD_DIM
    qscale = HEAD_DIM ** -0.5 * LOG2E
    w_in = even_w_in[0]
    col_scale = jnp.concatenate([
        jnp.full((a_cols,), qscale, F32), jnp.ones((2 * a_cols,), F32),
        jnp.full((b_cols,), qscale, F32), jnp.ones((2 * b_cols,), F32)])
    proj = _norm_proj(x0, attn_norm[0], (w_in * col_scale).astype(BF16), tm)
    oa = _dilated(proj, t5_table)
    ob = _natten(proj, na_rpb[0].reshape(-1), 3 * a_cols // LANES)
    w_out = even_w_out[0].astype(BF16)
    x1 = _mix_ffn(x0, [oa, ob], [w_out[:a_cols], w_out[a_cols:]], ffn_norm[0],
                  ffn_w_gate[0].astype(BF16), ffn_w_up[0].astype(BF16), ffn_w_down[0].astype(BF16),
                  None, tm)

    cscale = (QK_NOPE + QK_ROPE) ** -0.5 * LOG2E
    w_in1 = odd_w_in[0]
    win = jnp.concatenate([
        w_in1[:, :Q_LORA + KV_LORA],
        _per_head_lanes(w_in1[:, Q_LORA + KV_LORA:], 1, QK_ROPE, ROPE_LANE0)], axis=1).astype(BF16)
    wuq = (_per_head_lanes(mla_w_uq[0], C_HEADS, QK_NOPE + QK_ROPE) * cscale).astype(BF16)
    wuk = _per_head_lanes(mla_w_uk[0], C_HEADS, QK_NOPE).astype(BF16)
    wuv = _per_head_lanes(mla_w_uv[0], C_HEADS, V_DIM).astype(BF16)
    cos_t, sin_lo, sin_hi = _rope_tables(s)
    q1, k1, v1 = _mla_prep(x1, attn_norm[1], win, mla_q_norm[0], mla_kv_norm[0], wuq, wuk, wuv,
                           cos_t, sin_lo, sin_hi, tm)
    oc = _mla_attn(q1, k1, v1, t5_table)
    out = _mix_ffn(x1, [oc], [odd_w_out[0].astype(BF16)], ffn_norm[1],
                   ffn_w_gate[1].astype(BF16), ffn_w_up[1].astype(BF16), ffn_w_down[1].astype(BF16),
                   final_norm, tm)
    return out.reshape(b, s, d)
```

```python
import functools
import math

import numpy as np
import jax
import jax.numpy as jnp
from jax import lax
from jax.experimental import pallas as pl
from jax.experimental.pallas import tpu as pltpu

F32 = jnp.float32
BF16 = jnp.bfloat16

HEAD_DIM = 64
GRID_W = 64
RMS_EPS = 1e-6
A_HEADS = 8
B_HEADS = 8
DILATED_BRANCHES = ((128, 1), (512, 4), (2048, 16))
NA_ROWS = 8
NA_COLS = 16
C_HEADS = 16
Q_LORA = 384
KV_LORA = 128
QK_NOPE = 64
QK_ROPE = 32
V_DIM = 64
ROPE_THETA = 10000.0
T5_BUCKETS = 32
T5_MAX_DIST = 1024

LOG2E = math.log2(math.e)
NEG = -1e30
LANES = 128
VMEM_LIMIT = 62 * 1024 * 1024


def _t5_upper_bounds():
    half = T5_BUCKETS // 2
    max_exact = half // 2
    n = np.arange(0, 4 * T5_MAX_DIST, dtype=np.int64)
    nf = np.maximum(n, max_exact).astype(np.float32)
    val = (np.log(nf / np.float32(max_exact)) / np.float32(math.log(T5_MAX_DIST / max_exact))
           * np.float32(half - max_exact))
    large = np.minimum(max_exact + val.astype(np.int32), half - 1)
    bucket = np.where(n < max_exact, n, large)
    return tuple(int(np.argmax(bucket > b)) for b in range(half - 1))


T5_UPPER = _t5_upper_bounds()
T5_FAR = T5_UPPER[-1]


def _t5_bias_tile(t5_ref, head, rel):
    half = T5_BUCKETS // 2
    n = jnp.abs(rel)
    vneg = jnp.full(rel.shape, t5_ref[half - 1, head], F32)
    vpos = jnp.full(rel.shape, t5_ref[2 * half - 1, head], F32)
    for b in range(half - 2, -1, -1):
        inb = n < T5_UPPER[b]
        vneg = jnp.where(inb, t5_ref[b, head], vneg)
        vpos = jnp.where(inb, t5_ref[half + b, head], vpos)
    return jnp.where(rel > 0, vpos, vneg) * LOG2E


def _rms(x, g):
    return x * lax.rsqrt(jnp.mean(x * x, axis=-1, keepdims=True) + RMS_EPS) * g


def _nt_dot(a, b):
    return lax.dot_general(a, b, (((1,), (1,)), ((), ())), preferred_element_type=F32)


def _resident(shape, index_map):
    return pl.BlockSpec(shape, index_map, pipeline_mode=pl.Buffered(1))


def _norm_proj_kernel(x_ref, g_ref, w_ref, o_ref):
    h = _rms(x_ref[...], g_ref[...]).astype(BF16)
    o_ref[...] = jnp.dot(h, w_ref[...], preferred_element_type=F32).astype(o_ref.dtype)


def _norm_proj(x, g, w, tm):
    s, d = x.shape
    n = w.shape[1]
    return pl.pallas_call(
        _norm_proj_kernel,
        grid=(s // tm,),
        in_specs=[
            pl.BlockSpec((tm, d), lambda i: (i, 0)),
            _resident((1, d), lambda i: (0, 0)),
            _resident((d, n), lambda i: (0, 0)),
        ],
        out_specs=pl.BlockSpec((tm, n), lambda i: (i, 0)),
        out_shape=jax.ShapeDtypeStruct((s, n), BF16),
        compiler_params=pltpu.CompilerParams(
            dimension_semantics=("parallel",), vmem_limit_bytes=VMEM_LIMIT),
        name="norm_proj",
    )(x, g.reshape(1, d), w)


DIL_REACH = max(w // 2 for w, _ in DILATED_BRANCHES)
DIL_TQ = 256
DIL_W = DIL_TQ + 2 * DIL_REACH
DIL_BT = 256
DIL_WT = DIL_W // DIL_BT
DIL_DMAX = -(-(DIL_REACH + 1) // DIL_BT)


def _dilated_kernel(t5_ref, q_ref, k_ref, v_ref, o_ref, bias_ref, *, seq):
    hp = pl.program_id(0)
    i = pl.program_id(1)

    @pl.when(i == 0)
    def _():
        row = lax.broadcasted_iota(jnp.int32, (DIL_TQ, DIL_BT), 0)
        col = lax.broadcasted_iota(jnp.int32, (DIL_TQ, DIL_BT), 1)
        for d in range(-DIL_DMAX, DIL_DMAX + 1):
            rel = col - row + d * DIL_BT
            n = jnp.abs(rel)
            count = jnp.zeros(rel.shape, jnp.int32)
            for window, dil in DILATED_BRANCHES:
                member = ((rel & (dil - 1)) == 0) & (n <= (window // 2))
                count = count + member.astype(jnp.int32)
            logmult = jnp.where(count == 3, math.log2(3.0), jnp.where(count == 2, 1.0, 0.0))
            for hh in range(2):
                b = _t5_bias_tile(t5_ref, 2 * hp + hh, rel) + logmult
                bias_ref[hh, d + DIL_DMAX] = jnp.where(count > 0, b, NEG)

    t0 = i * DIL_TQ
    start = pl.multiple_of(jnp.clip(t0 - DIL_REACH, 0, seq - DIL_W), DIL_BT)
    d0 = (start - t0) // DIL_BT
    kw = k_ref[pl.ds(start, DIL_W), :]
    vw = v_ref[pl.ds(start, DIL_W), :]
    q = q_ref[...]
    lane = lax.broadcasted_iota(jnp.int32, q.shape, 1)

    outs = []
    for hh in range(2):
        in_head = (lane >= hh * HEAD_DIM) & (lane < (hh + 1) * HEAD_DIM)
        qm = jnp.where(in_head, q, jnp.zeros_like(q))
        bias = jnp.concatenate(
            [bias_ref[hh, jnp.clip(d0 + k, -DIL_DMAX, DIL_DMAX) + DIL_DMAX]
             for k in range(DIL_WT)], axis=1)
        s = _nt_dot(qm, kw) + bias
        m = jnp.max(s, axis=-1, keepdims=True)
        p = jnp.exp2(s - m)
        l = jnp.sum(p, axis=-1, keepdims=True)
        o = jnp.dot(p.astype(BF16), vw, preferred_element_type=F32)
        outs.append(o / l)
    o_ref[...] = jnp.where(lane < HEAD_DIM, outs[0], outs[1]).astype(o_ref.dtype)


def _dilated(proj, t5_table):
    s = proj.shape[0]
    pairs = A_HEADS // 2
    assert s % DIL_BT == 0 and s >= DIL_W and DIL_W % DIL_BT == 0 and DIL_REACH % DIL_BT == 0
    return pl.pallas_call(
        functools.partial(_dilated_kernel, seq=s),
        grid=(pairs, s // DIL_TQ),
        in_specs=[
            pl.BlockSpec(memory_space=pltpu.SMEM),
            pl.BlockSpec((DIL_TQ, LANES), lambda h, i: (i, h)),
            _resident((s, LANES), lambda h, i: (0, pairs + h)),
            _resident((s, LANES), lambda h, i: (0, 2 * pairs + h)),
        ],
        out_specs=pl.BlockSpec((DIL_TQ, LANES), lambda h, i: (i, h)),
        out_shape=jax.ShapeDtypeStruct((s, A_HEADS * HEAD_DIM), BF16),
        scratch_shapes=[pltpu.VMEM((2, 2 * DIL_DMAX + 1, DIL_TQ, DIL_BT), F32)],
        compiler_params=pltpu.CompilerParams(
            dimension_semantics=("arbitrary", "arbitrary"), vmem_limit_bytes=VMEM_LIMIT),
        name="dilated",
    )(t5_table, proj, proj, proj)


NA_QROWS = 8
NA_KROWS = 16
NA_TQ = NA_QROWS * GRID_W
NA_TK = NA_KROWS * GRID_W
NA_RO = 2 * NA_ROWS - 1
NA_CO = 2 * NA_COLS - 1


NA_NB = 2
NA_ROWS_CHUNK = 32
NA_EDGE, NA_INNER = 0, 1


def _natten_kernel(rpb_ref, q_ref, qn_ref, k_ref, v_ref, o_ref,
                   tile_ref, bias_ref, s_ref, p_ref, *, rows):
    hp = pl.program_id(0)
    i = pl.program_id(1)
    n_steps = pl.num_programs(1)
    n_blocks = rows // NA_QROWS

    def key_row0(blk):
        return jnp.clip(blk * NA_QROWS - NA_ROWS // 2, 0, rows - NA_KROWS)

    def build_tiles():
        c = lax.broadcasted_iota(jnp.int32, (GRID_W, LANES), 0)
        kc = lax.broadcasted_iota(jnp.int32, (GRID_W, LANES), 1) & (GRID_W - 1)
        co = jnp.clip(kc - c + (NA_COLS - 1), 0, NA_CO - 1)
        c0 = jnp.clip(c - NA_COLS // 2, 0, GRID_W - NA_COLS)
        col_ok = (kc >= c0) & (kc < c0 + NA_COLS)
        for hh in range(2):
            head = 2 * hp + hh

            def body(ro, carry):
                base = (head * NA_RO + ro) * NA_CO
                v = jnp.full(co.shape, rpb_ref[base + NA_CO - 1], F32)
                for k in range(NA_CO - 2, -1, -1):
                    v = jnp.where(co == k, rpb_ref[base + k], v)
                tile_ref[hh, ro] = jnp.where(col_ok, v * LOG2E, NEG)
                return carry

            lax.fori_loop(0, NA_RO, body, 0)

    def build_bias(blk, which):
        rb = blk * NA_QROWS
        kb0 = key_row0(blk)
        left = lax.broadcasted_iota(jnp.int32, (GRID_W, LANES), 1) < GRID_W
        for qr in range(NA_QROWS):
            r = rb + qr
            r0 = jnp.clip(r - NA_ROWS // 2, 0, rows - NA_ROWS)
            for a in range(NA_KROWS // 2):
                kr = kb0 + 2 * a
                ro = kr - r + (NA_ROWS - 1)
                ok0 = (kr >= r0) & (kr < r0 + NA_ROWS)
                ok1 = (kr + 1 >= r0) & (kr + 1 < r0 + NA_ROWS)
                ro0 = jnp.clip(ro, 0, NA_RO - 1)
                ro1 = jnp.clip(ro + 1, 0, NA_RO - 1)
                for hh in range(2):
                    t0 = jnp.where(ok0, tile_ref[hh, ro0], NEG)
                    t1 = jnp.where(ok1, tile_ref[hh, ro1], NEG)
                    bias_ref[which, hh, qr * GRID_W:(qr + 1) * GRID_W,
                             a * LANES:(a + 1) * LANES] = jnp.where(left, t0, t1)

    @pl.when(i == 0)
    def _():
        build_tiles()
        build_bias(0, NA_EDGE)
        build_bias(1, NA_INNER)

    @pl.when(i == n_steps - 1)
    def _():
        build_bias(n_blocks - 1, NA_EDGE)

    lane = lax.broadcasted_iota(jnp.int32, (NA_TQ, LANES), 1)
    vlane = lax.broadcasted_iota(jnp.int16, (NA_TK, LANES), 1)

    def key_start(blk):
        return pl.multiple_of(key_row0(blk) * GRID_W, GRID_W * 4)

    def scores(blk, slot, q):
        kw = k_ref[pl.ds(key_start(blk), NA_TK), :]
        for hh in range(2):
            in_head = (lane >= hh * HEAD_DIM) & (lane < (hh + 1) * HEAD_DIM)
            s_ref[slot, hh] = _nt_dot(jnp.where(in_head, q, jnp.zeros_like(q)), kw)

    def softmax(blk, slot):
        which = jnp.where((blk == 0) | (blk == n_blocks - 1), NA_EDGE, NA_INNER)
        for hh in range(2):
            for r0 in range(0, NA_TQ, NA_ROWS_CHUNK):
                rws = slice(r0, r0 + NA_ROWS_CHUNK)
                s = s_ref[slot, hh, rws, :] + bias_ref[which, hh, rws, :]
                m = jnp.max(s, axis=-1, keepdims=True)
                p_ref[slot, hh, rws, :] = jnp.exp2(s - m).astype(BF16)

    def pv(blk, slot):
        vw = v_ref[pl.ds(key_start(blk), NA_TK), :]
        one = jnp.ones_like(vw)
        zero = jnp.zeros_like(vw)
        v0 = jnp.where(vlane < HEAD_DIM, vw, jnp.where(vlane == HEAD_DIM, one, zero))
        v1 = jnp.where(vlane >= HEAD_DIM, vw, jnp.where(vlane == 0, one, zero))
        o0 = jnp.dot(p_ref[slot, 0], v0, preferred_element_type=F32)
        o1 = jnp.dot(p_ref[slot, 1], v1, preferred_element_type=F32)
        out = jnp.where(lane < HEAD_DIM, o0 / o0[:, HEAD_DIM:HEAD_DIM + 1], o1 / o1[:, 0:1])
        return out.astype(o_ref.dtype)

    first = NA_NB * i
    nxt = NA_NB * jnp.minimum(i + 1, n_steps - 1)

    def q_rows(j):
        src, jj = (q_ref, j) if j < NA_NB else (qn_ref, j - NA_NB)
        return src[jj * NA_TQ:(jj + 1) * NA_TQ]

    def block(j):
        return first + j if j < NA_NB else nxt + (j - NA_NB)

    @pl.when(i == 0)
    def _():
        scores(block(0), 0, q_rows(0))
        scores(block(1), 1, q_rows(1))
        softmax(block(0), 0)

    for j in range(NA_NB):
        scores(block(j + 2), j % 2, q_rows(j + 2))
        softmax(block(j + 1), (j + 1) % 2)
        o_ref[j * NA_TQ:(j + 1) * NA_TQ] = pv(block(j), j % 2)


def _natten(proj, rpb_flat, col0):
    s = proj.shape[0]
    rows = s // GRID_W
    pairs = B_HEADS // 2
    assert s % (NA_NB * NA_TQ) == 0 and NA_NB % 2 == 0 and rows // NA_QROWS >= 3
    n_steps = s // (NA_NB * NA_TQ)
    return pl.pallas_call(
        functools.partial(_natten_kernel, rows=rows),
        grid=(pairs, n_steps),
        in_specs=[
            pl.BlockSpec(memory_space=pltpu.SMEM),
            pl.BlockSpec((NA_NB * NA_TQ, LANES), lambda h, i: (i, col0 + h)),
            pl.BlockSpec((NA_NB * NA_TQ, LANES),
                         lambda h, i: (jnp.minimum(i + 1, n_steps - 1), col0 + h)),
            _resident((s, LANES), lambda h, i: (0, col0 + pairs + h)),
            _resident((s, LANES), lambda h, i: (0, col0 + 2 * pairs + h)),
        ],
        out_specs=pl.BlockSpec((NA_NB * NA_TQ, LANES), lambda h, i: (i, h)),
        out_shape=jax.ShapeDtypeStruct((s, B_HEADS * HEAD_DIM), BF16),
        scratch_shapes=[
            pltpu.VMEM((2, NA_RO, GRID_W, LANES), F32),
            pltpu.VMEM((2, 2, NA_TQ, NA_TK), F32),
            pltpu.VMEM((2, 2, NA_TQ, NA_TK), F32),
            pltpu.VMEM((2, 2, NA_TQ, NA_TK), BF16),
        ],
        compiler_params=pltpu.CompilerParams(
            dimension_semantics=("arbitrary", "arbitrary"), vmem_limit_bytes=VMEM_LIMIT),
        name="natten",
    )(rpb_flat, proj, proj, proj, proj)


FF_CHUNK = 256


def _mix_ffn_kernel(*refs, n_o, final):
    x_ref = refs[0]
    o_refs = refs[1:1 + n_o]
    wo_refs = refs[1 + n_o:1 + 2 * n_o]
    g_ref, wg_ref, wu_ref, wd_ref = refs[1 + 2 * n_o:5 + 2 * n_o]
    rest = refs[5 + 2 * n_o:]
    gf_ref = rest[0] if final else None
    out_ref = rest[-1]

    y = x_ref[...]
    for o_ref, wo_ref in zip(o_refs, wo_refs):
        y = y + jnp.dot(o_ref[...], wo_ref[...], preferred_element_type=F32)
    h = _rms(y, g_ref[...]).astype(BF16)
    d_ff = wg_ref.shape[1]
    acc = jnp.zeros(y.shape, F32)
    for c in range(0, d_ff, FF_CHUNK):
        gate = jnp.dot(h, wg_ref[:, c:c + FF_CHUNK], preferred_element_type=F32)
        up = jnp.dot(h, wu_ref[:, c:c + FF_CHUNK], preferred_element_type=F32)
        act = (gate * jax.nn.sigmoid(gate) * up).astype(BF16)
        acc = acc + jnp.dot(act, wd_ref[c:c + FF_CHUNK, :], preferred_element_type=F32)
    y = y + acc
    if final:
        y = _rms(y, gf_ref[...])
    out_ref[...] = y


def _mix_ffn(x, o_parts, wo_parts, g, wg, wu, wd, gf, tm):
    s, d = x.shape
    d_ff = wg.shape[1]
    assert d_ff % FF_CHUNK == 0
    n_o = len(o_parts)
    final = gf is not None
    in_specs = [pl.BlockSpec((tm, d), lambda i: (i, 0))]
    in_specs += [pl.BlockSpec((tm, o.shape[1]), lambda i: (i, 0)) for o in o_parts]
    in_specs += [_resident(w.shape, lambda i: (0, 0)) for w in wo_parts]
    in_specs += [
        _resident((1, d), lambda i: (0, 0)),
        _resident((d, d_ff), lambda i: (0, 0)),
        _resident((d, d_ff), lambda i: (0, 0)),
        _resident((d_ff, d), lambda i: (0, 0)),
    ]
    args = [x, *o_parts, *wo_parts, g.reshape(1, d), wg, wu, wd]
    if final:
        in_specs.append(_resident((1, d), lambda i: (0, 0)))
        args.append(gf.reshape(1, d))
    return pl.pallas_call(
        functools.partial(_mix_ffn_kernel, n_o=n_o, final=final),
        grid=(s // tm,),
        in_specs=in_specs,
        out_specs=pl.BlockSpec((tm, d), lambda i: (i, 0)),
        out_shape=jax.ShapeDtypeStruct((s, d), F32),
        compiler_params=pltpu.CompilerParams(
            dimension_semantics=("parallel",), vmem_limit_bytes=VMEM_LIMIT),
        name="mix_ffn",
    )(*args)


ROPE_LANE0 = QK_NOPE
ROPE_HALF = QK_ROPE // 2


def _rope(x, cos_t, sin_lo, sin_hi):
    return (x * cos_t + pltpu.roll(x, LANES - ROPE_HALF, 1) * sin_lo
            + pltpu.roll(x, ROPE_HALF, 1) * sin_hi)


def _mla_prep_kernel(x_ref, g_ref, win_ref, qn_ref, kvn_ref, wuq_ref, wuk_ref, wuv_ref,
                     cos_ref, slo_ref, shi_ref, q_out, k_out, v_out):
    h = _rms(x_ref[...], g_ref[...]).astype(BF16)
    proj = jnp.dot(h, win_ref[...], preferred_element_type=F32)
    cq = _rms(proj[:, :Q_LORA], qn_ref[...]).astype(BF16)
    ckv = _rms(proj[:, Q_LORA:Q_LORA + KV_LORA], kvn_ref[...]).astype(BF16)
    cos_t, sin_lo, sin_hi = cos_ref[...], slo_ref[...], shi_ref[...]
    kr = _rope(proj[:, Q_LORA + KV_LORA:], cos_t, sin_lo, sin_hi)
    q = jnp.dot(cq, wuq_ref[...], preferred_element_type=F32)
    kn = jnp.dot(ckv, wuk_ref[...], preferred_element_type=F32)
    v = jnp.dot(ckv, wuv_ref[...], preferred_element_type=F32)
    lane = lax.broadcasted_iota(jnp.int32, (1, LANES), 1)
    one_col = jnp.where(lane == V_DIM, 1.0, 0.0)
    for hd in range(C_HEADS):
        sl = slice(hd * LANES, (hd + 1) * LANES)
        q_out[hd] = _rope(q[:, sl], cos_t, sin_lo, sin_hi).astype(BF16)
        k_out[hd] = (kn[:, sl] + kr).astype(BF16)
        v_out[hd] = (v[:, sl] + one_col).astype(BF16)


def _mla_prep(x, g, win, qn, kvn, wuq, wuk, wuv, cos_t, sin_lo, sin_hi, tm):
    s, d = x.shape
    const = lambda i: (0, 0)
    row = lambda i: (i, 0)
    head_out = jax.ShapeDtypeStruct((C_HEADS, s, LANES), BF16)
    head_spec = pl.BlockSpec((C_HEADS, tm, LANES), lambda i: (0, i, 0))
    return pl.pallas_call(
        _mla_prep_kernel,
        grid=(s // tm,),
        in_specs=[
            pl.BlockSpec((tm, d), row),
            _resident((1, d), const),
            _resident(win.shape, const),
            _resident((1, Q_LORA), const),
            _resident((1, KV_LORA), const),
            _resident(wuq.shape, const),
            _resident(wuk.shape, const),
            _resident(wuv.shape, const),
            pl.BlockSpec((tm, LANES), row),
            pl.BlockSpec((tm, LANES), row),
            pl.BlockSpec((tm, LANES), row),
        ],
        out_specs=[head_spec, head_spec, head_spec],
        out_shape=[head_out, head_out, head_out],
        compiler_params=pltpu.CompilerParams(
            dimension_semantics=("parallel",), vmem_limit_bytes=VMEM_LIMIT),
        name="mla_prep",
    )(x, g.reshape(1, d), win, qn.reshape(1, -1), kvn.reshape(1, -1), wuq, wuk, wuv,
      cos_t, sin_lo, sin_hi)


MLA_TQ = 512
MLA_TK = 512
MLA_SLAB = 2048
MLA_ROWS = 16
MLA_BAND = (T5_FAR - 1 + MLA_TK - 1) // MLA_TK
MLA_FAR = MLA_BAND + 1


def _mla_attn_kernel(t5_ref, q_ref, qn_ref, k_ref, v_ref, o_ref,
                     bias_ref, s_ref, p_ref, m_ref, acc_ref, *, seq):
    hp = pl.program_id(0)
    qi = pl.program_id(1)
    n_slab = seq // MLA_SLAB
    tiles = MLA_SLAB // MLA_TK
    half = T5_BUCKETS // 2
    far_neg = [t5_ref[half - 1, 2 * hp + hh] * LOG2E for hh in range(2)]
    far_pos = [t5_ref[2 * half - 1, 2 * hp + hh] * LOG2E for hh in range(2)]

    @pl.when(qi == 0)
    def _():
        row = lax.broadcasted_iota(jnp.int32, (MLA_TQ, MLA_TK), 0)
        col = lax.broadcasted_iota(jnp.int32, (MLA_TQ, MLA_TK), 1)
        for hh in range(2):
            bias_ref[hh, 0] = jnp.full((MLA_TQ, MLA_TK), far_neg[hh], F32)
            bias_ref[hh, 2 * MLA_FAR] = jnp.full((MLA_TQ, MLA_TK), far_pos[hh], F32)
            for d in range(-MLA_BAND, MLA_BAND + 1):
                bias_ref[hh, d + MLA_FAR] = _t5_bias_tile(
                    t5_ref, 2 * hp + hh, col - row + d * MLA_TK)

    m_ref[...] = jnp.full(m_ref.shape, NEG, F32)
    acc_ref[...] = jnp.zeros(acc_ref.shape, F32)

    def scores(slab, slot, q_src):
        kstart = pl.multiple_of(slab * MLA_SLAB, MLA_SLAB)
        for hh in range(2):
            s_ref[slot, hh] = _nt_dot(q_src[hh], k_ref[hh, pl.ds(kstart, MLA_SLAB), :])

    def softmax_pv(slab, slot, banded):
        kstart = pl.multiple_of(slab * MLA_SLAB, MLA_SLAB)
        for hh in range(2):
            if banded:
                idx = [jnp.clip(slab * tiles + u - qi, -MLA_FAR, MLA_FAR) + MLA_FAR
                       for u in range(tiles)]
            else:
                c = jnp.where(slab * tiles < qi, far_neg[hh], far_pos[hh])
            alphas = []
            for r0 in range(0, MLA_TQ, MLA_ROWS):
                rows = slice(r0, r0 + MLA_ROWS)
                s = s_ref[slot, hh, rows, :]
                if banded:
                    s = s + jnp.concatenate(
                        [bias_ref[hh, idx[u], rows, :] for u in range(tiles)], axis=1)
                    m_blk = jnp.max(s, axis=-1, keepdims=True)
                else:
                    m_blk = jnp.max(s, axis=-1, keepdims=True) + c
                m_old = m_ref[hh, rows, :]
                m_new = jnp.maximum(m_old, m_blk)
                m_ref[hh, rows, :] = m_new
                alphas.append(jnp.exp2(m_old - m_new))
                shift = m_new if banded else m_new - c
                p_ref[hh, rows, :] = jnp.exp2(s - shift).astype(BF16)
            alpha = jnp.concatenate(alphas, axis=0)
            acc_ref[hh] = alpha * acc_ref[hh] + jnp.dot(
                p_ref[hh], v_ref[hh, pl.ds(kstart, MLA_SLAB), :], preferred_element_type=F32)

    def work(slab, slot, nxt, banded):
        scores(nxt[0], 1 - slot, nxt[1])
        softmax_pv(slab, slot, banded)

    def is_banded(slab):
        d0 = slab * tiles - qi
        return (d0 >= -MLA_BAND - (tiles - 1)) & (d0 <= MLA_BAND)

    def stage(slab, slot, nxt):
        banded = is_banded(slab)
        pl.when(banded)(functools.partial(work, slab, slot, nxt, True))
        pl.when(jnp.logical_not(banded))(functools.partial(work, slab, slot, nxt, False))

    def stage_pair(slab, nxt):
        stage(slab, 0, (slab + 1, q_ref))
        stage(slab + 1, 1, nxt)

    pl.when(qi == 0)(functools.partial(scores, 0, 0, q_ref))

    def loop_body(t, carry):
        stage_pair(2 * t, (2 * t + 2, q_ref))
        return carry

    lax.fori_loop(0, n_slab // 2 - 1, loop_body, 0)
    stage_pair(n_slab - 2, (0, qn_ref))

    lane = lax.broadcasted_iota(jnp.int32, (MLA_TQ, LANES), 1)
    outs = [acc_ref[hh] / acc_ref[hh, :, V_DIM:V_DIM + 1] for hh in range(2)]
    out = jnp.where(lane < V_DIM, outs[0], pltpu.roll(outs[1], V_DIM, 1))
    o_ref[...] = out.astype(o_ref.dtype)


def _mla_attn(q, k, v, t5_table):
    heads, s, _ = q.shape
    pairs = heads // 2
    assert MLA_TQ == MLA_TK and s % (2 * MLA_SLAB) == 0 and MLA_SLAB % MLA_TK == 0
    nq = s // MLA_TQ
    return pl.pallas_call(
        functools.partial(_mla_attn_kernel, seq=s),
        grid=(pairs, nq),
        in_specs=[
            pl.BlockSpec(memory_space=pltpu.SMEM),
            pl.BlockSpec((2, MLA_TQ, LANES), lambda h, i: (h, i, 0)),
            pl.BlockSpec((2, MLA_TQ, LANES), lambda h, i: (h, jnp.minimum(i + 1, nq - 1), 0)),
            _resident((2, s, LANES), lambda h, i: (h, 0, 0)),
            _resident((2, s, LANES), lambda h, i: (h, 0, 0)),
        ],
        out_specs=pl.BlockSpec((MLA_TQ, LANES), lambda h, i: (i, h)),
        out_shape=jax.ShapeDtypeStruct((s, heads * V_DIM), BF16),
        scratch_shapes=[
            pltpu.VMEM((2, 2 * MLA_FAR + 1, MLA_TQ, MLA_TK), F32),
            pltpu.VMEM((2, 2, MLA_TQ, MLA_SLAB), F32),
            pltpu.VMEM((2, MLA_TQ, MLA_SLAB), BF16),
            pltpu.VMEM((2, MLA_TQ, 1), F32),
            pltpu.VMEM((2, MLA_TQ, LANES), F32),
        ],
        compiler_params=pltpu.CompilerParams(
            dimension_semantics=("arbitrary", "arbitrary"), vmem_limit_bytes=VMEM_LIMIT),
        name="mla_attn",
    )(t5_table, q, q, k, v)


def _per_head_lanes(w, heads, width, lane0=0):
    k = w.shape[0]
    w = w.reshape(k, heads, width)
    w = jnp.pad(w, ((0, 0), (0, 0), (lane0, LANES - lane0 - width)))
    return w.reshape(k, heads * LANES)


def _rope_tables(s):
    pos = jnp.arange(s, dtype=F32)
    freqs = ROPE_THETA ** (-jnp.arange(0, QK_ROPE, 2, dtype=F32) / QK_ROPE)
    ang = pos[:, None] * freqs[None, :]
    cos, sin = jnp.cos(ang), jnp.sin(ang)
    zeros = jnp.zeros((s, ROPE_HALF), F32)
    tail = LANES - ROPE_LANE0 - QK_ROPE
    cos_t = jnp.concatenate([jnp.ones((s, ROPE_LANE0), F32), cos, cos, jnp.ones((s, tail), F32)], axis=1)
    sin_lo = jnp.concatenate([jnp.zeros((s, ROPE_LANE0), F32), -sin, zeros, jnp.zeros((s, tail), F32)], axis=1)
    sin_hi = jnp.concatenate([jnp.zeros((s, ROPE_LANE0), F32), zeros, sin, jnp.zeros((s, tail), F32)], axis=1)
    return cos_t, sin_lo, sin_hi


def kernel(x, t5_table, attn_norm, ffn_norm, even_w_in, na_rpb, even_w_out, odd_w_in, mla_q_norm,
           mla_w_uq, mla_kv_norm, mla_w_uk, mla_w_uv, odd_w_out, ffn_w_gate, ffn_w_up, ffn_w_down,
           final_norm):
    b, s, d = x.shape
    assert b == 1 and s % (GRID_W * NA_KROWS) == 0 and s % MLA_TQ == 0
    x0 = x.reshape(s, d)
    tm = 512

    a_cols = A_HEADS * HEAD_DIM
    b_cols = B_HEADS * HEAD_DIM
    qscale = HEAD_DIM ** -0.5 * LOG2E
    w_in = even_w_in[0]
    col_scale = jnp.concatenate([
        jnp.full((a_cols,), qscale, F32), jnp.ones((2 * a_cols,), F32),
        jnp.full((b_cols,), qscale, F32), jnp.ones((2 * b_cols,), F32)])
    proj = _norm_proj(x0, attn_norm[0], (w_in * col_scale).astype(BF16), tm)
    oa = _dilated(proj, t5_table)
    ob = _natten(proj, na_rpb[0].reshape(-1), 3 * a_cols // LANES)
    w_out = even_w_out[0].astype(BF16)
    x1 = _mix_ffn(x0, [oa, ob], [w_out[:a_cols], w_out[a_cols:]], ffn_norm[0],
                  ffn_w_gate[0].astype(BF16), ffn_w_up[0].astype(BF16), ffn_w_down[0].astype(BF16),
                  None, tm)

    cscale = (QK_NOPE + QK_ROPE) ** -0.5 * LOG2E
    w_in1 = odd_w_in[0]
    win = jnp.concatenate([
        w_in1[:, :Q_LORA + KV_LORA],
        _per_head_lanes(w_in1[:, Q_LORA + KV_LORA:], 1, QK_ROPE, ROPE_LANE0)], axis=1).astype(BF16)
    wuq = (_per_head_lanes(mla_w_uq[0], C_HEADS, QK_NOPE + QK_ROPE) * cscale).astype(BF16)
    wuk = _per_head_lanes(mla_w_uk[0], C_HEADS, QK_NOPE).astype(BF16)
    wuv = _per_head_lanes(mla_w_uv[0], C_HEADS, V_DIM).astype(BF16)
    cos_t, sin_lo, sin_hi = _rope_tables(s)
    q1, k1, v1 = _mla_prep(x1, attn_norm[1], win, mla_q_norm[0], mla_kv_norm[0], wuq, wuk, wuv,
                           cos_t, sin_lo, sin_hi, tm)
    oc = _mla_attn(q1, k1, v1, t5_table)
    out = _mix_ffn(x1, [oc], [odd_w_out[0].astype(BF16)], ffn_norm[1],
                   ffn_w_gate[1].astype(BF16), ffn_w_up[1].astype(BF16), ffn_w_down[1].astype(BF16),
                   final_norm, tm)
    return out.reshape(b, s, d)
```

```python
import functools
import math

import numpy as np
import jax
import jax.numpy as jnp
from jax import lax
from jax.experimental import pallas as pl
from jax.experimental.pallas import tpu as pltpu

F32 = jnp.float32
BF16 = jnp.bfloat16

HEAD_DIM = 64
GRID_W = 64
RMS_EPS = 1e-6
A_HEADS = 8
B_HEADS = 8
DILATED_BRANCHES = ((128, 1), (512, 4), (2048, 16))
NA_ROWS = 8
NA_COLS = 16
C_HEADS = 16
Q_LORA = 384
KV_LORA = 128
QK_NOPE = 64
QK_ROPE = 32
V_DIM = 64
ROPE_THETA = 10000.0
T5_BUCKETS = 32
T5_MAX_DIST = 1024

LOG2E = math.log2(math.e)
NEG = -1e30
LANES = 128
VMEM_LIMIT = 62 * 1024 * 1024


def _t5_upper_bounds():
    half = T5_BUCKETS // 2
    max_exact = half // 2
    n = np.arange(0, 4 * T5_MAX_DIST, dtype=np.int64)
    nf = np.maximum(n, max_exact).astype(np.float32)
    val = (np.log(nf / np.float32(max_exact)) / np.float32(math.log(T5_MAX_DIST / max_exact))
           * np.float32(half - max_exact))
    large = np.minimum(max_exact + val.astype(np.int32), half - 1)
    bucket = np.where(n < max_exact, n, large)
    return tuple(int(np.argmax(bucket > b)) for b in range(half - 1))


T5_UPPER = _t5_upper_bounds()
T5_FAR = T5_UPPER[-1]


def _t5_bias_tile(t5_ref, head, rel):
    half = T5_BUCKETS // 2
    n = jnp.abs(rel)
    vneg = jnp.full(rel.shape, t5_ref[half - 1, head], F32)
    vpos = jnp.full(rel.shape, t5_ref[2 * half - 1, head], F32)
    for b in range(half - 2, -1, -1):
        inb = n < T5_UPPER[b]
        vneg = jnp.where(inb, t5_ref[b, head], vneg)
        vpos = jnp.where(inb, t5_ref[half + b, head], vpos)
    return jnp.where(rel > 0, vpos, vneg) * LOG2E


def _rms(x, g):
    return x * lax.rsqrt(jnp.mean(x * x, axis=-1, keepdims=True) + RMS_EPS) * g


def _nt_dot(a, b):
    return lax.dot_general(a, b, (((1,), (1,)), ((), ())), preferred_element_type=F32)


def _resident(shape, index_map):
    return pl.BlockSpec(shape, index_map, pipeline_mode=pl.Buffered(1))


def _norm_proj_kernel(x_ref, g_ref, w_ref, o_ref):
    h = _rms(x_ref[...], g_ref[...]).astype(BF16)
    o_ref[...] = jnp.dot(h, w_ref[...], preferred_element_type=F32).astype(o_ref.dtype)


def _norm_proj(x, g, w, tm):
    s, d = x.shape
    n = w.shape[1]
    return pl.pallas_call(
        _norm_proj_kernel,
        grid=(s // tm,),
        in_specs=[
            pl.BlockSpec((tm, d), lambda i: (i, 0)),
            _resident((1, d), lambda i: (0, 0)),
            _resident((d, n), lambda i: (0, 0)),
        ],
        out_specs=pl.BlockSpec((tm, n), lambda i: (i, 0)),
        out_shape=jax.ShapeDtypeStruct((s, n), BF16),
        compiler_params=pltpu.CompilerParams(
            dimension_semantics=("parallel",), vmem_limit_bytes=VMEM_LIMIT),
        name="norm_proj",
    )(x, g.reshape(1, d), w)


DIL_REACH = max(w // 2 for w, _ in DILATED_BRANCHES)
DIL_TQ = 256
DIL_NB = 2
DIL_ROWS = 16
DIL_W = DIL_TQ + 2 * DIL_REACH
DIL_BT = 256
DIL_WT = DIL_W // DIL_BT
DIL_DMAX = -(-(DIL_REACH + 1) // DIL_BT)


def _dilated_kernel(t5_ref, q_ref, qn_ref, k_ref, v_ref, o_ref, bias_ref, s_ref, p_ref, *, seq):
    hp = pl.program_id(0)
    i = pl.program_id(1)

    @pl.when(i == 0)
    def _():
        row = lax.broadcasted_iota(jnp.int32, (DIL_TQ, DIL_BT), 0)
        col = lax.broadcasted_iota(jnp.int32, (DIL_TQ, DIL_BT), 1)
        for d in range(-DIL_DMAX, DIL_DMAX + 1):
            rel = col - row + d * DIL_BT
            n = jnp.abs(rel)
            count = jnp.zeros(rel.shape, jnp.int32)
            for window, dil in DILATED_BRANCHES:
                member = ((rel & (dil - 1)) == 0) & (n <= (window // 2))
                count = count + member.astype(jnp.int32)
            logmult = jnp.where(count == 3, math.log2(3.0), jnp.where(count == 2, 1.0, 0.0))
            for hh in range(2):
                b = _t5_bias_tile(t5_ref, 2 * hp + hh, rel) + logmult
                bias_ref[hh, d + DIL_DMAX] = jnp.where(count > 0, b, NEG)

    lane = lax.broadcasted_iota(jnp.int32, (DIL_TQ, LANES), 1)
    vlane = lax.broadcasted_iota(jnp.int16, (DIL_W, LANES), 1)
    n_steps = pl.num_programs(1)

    def window(blk):
        t0 = blk * DIL_TQ
        start = pl.multiple_of(jnp.clip(t0 - DIL_REACH, 0, seq - DIL_W), DIL_BT)
        return start, (start - t0) // DIL_BT

    def scores(blk, slot, q, after=None):
        if after is not None:
            q = jnp.where(i < 0, jnp.broadcast_to(after, q.shape).astype(q.dtype), q)
        start, _ = window(blk)
        kw = k_ref[pl.ds(start, DIL_W), :]
        for hh in range(2):
            in_head = (lane >= hh * HEAD_DIM) & (lane < (hh + 1) * HEAD_DIM)
            s_ref[slot, hh] = _nt_dot(jnp.where(in_head, q, jnp.zeros_like(q)), kw)

    def softmax(blk, slot, after=None):
        _, d0 = window(blk)
        tile = [jnp.clip(d0 + k, -DIL_DMAX, DIL_DMAX) + DIL_DMAX for k in range(DIL_WT)]
        for hh in range(2):
            bias = jnp.concatenate([bias_ref[hh, tile[k]] for k in range(DIL_WT)], axis=1)
            s = s_ref[slot, hh] + bias
            m = jnp.max(s, axis=-1, keepdims=True)
            if after is not None:
                m = jnp.maximum(m, jnp.where(i < 0, after[:, hh:hh + 1].astype(F32), NEG))
            p_ref[slot, hh] = jnp.exp2(s - m).astype(BF16)
        return m

    def pv(blk, slot):
        start, _ = window(blk)
        vw = v_ref[pl.ds(start, DIL_W), :]
        one = jnp.ones_like(vw)
        zero = jnp.zeros_like(vw)
        v0 = jnp.where(vlane < HEAD_DIM, vw, jnp.where(vlane == HEAD_DIM, one, zero))
        v1 = jnp.where(vlane >= HEAD_DIM, vw, jnp.where(vlane == 0, one, zero))
        o0 = jnp.dot(p_ref[slot, 0], v0, preferred_element_type=F32)
        o1 = jnp.dot(p_ref[slot, 1], v1, preferred_element_type=F32)
        out = jnp.where(lane < HEAD_DIM, o0 / o0[:, HEAD_DIM:HEAD_DIM + 1], o1 / o1[:, 0:1])
        return out.astype(o_ref.dtype)

    first = DIL_NB * i
    nxt = DIL_NB * jnp.minimum(i + 1, n_steps - 1)

    def q_rows(j):
        src, jj = (q_ref, j) if j < DIL_NB else (qn_ref, j - DIL_NB)
        return src[jj * DIL_TQ:(jj + 1) * DIL_TQ]

    def block(j):
        return first + j if j < DIL_NB else nxt + (j - DIL_NB)

    @pl.when(i == 0)
    def _():
        scores(block(0), 0, q_rows(0))
        scores(block(1), 1, q_rows(1))
        softmax(block(0), 0)

    out = None
    m_prev = None
    for j in range(DIL_NB):
        scores(block(j + 2), j % 2, q_rows(j + 2), after=m_prev)
        m_prev = softmax(block(j + 1), (j + 1) % 2, after=out)
        out = pv(block(j), j % 2)
        o_ref[j * DIL_TQ:(j + 1) * DIL_TQ] = out


def _dilated(proj, t5_table):
    s = proj.shape[0]
    pairs = A_HEADS // 2
    assert s % (DIL_NB * DIL_TQ) == 0 and s >= DIL_W and DIL_NB % 2 == 0
    assert DIL_TQ % DIL_BT == 0 and DIL_REACH % DIL_BT == 0
    n_steps = s // (DIL_NB * DIL_TQ)
    return pl.pallas_call(
        functools.partial(_dilated_kernel, seq=s),
        grid=(pairs, n_steps),
        in_specs=[
            pl.BlockSpec(memory_space=pltpu.SMEM),
            pl.BlockSpec((DIL_NB * DIL_TQ, LANES), lambda h, i: (i, h)),
            pl.BlockSpec((DIL_NB * DIL_TQ, LANES),
                         lambda h, i: (jnp.minimum(i + 1, n_steps - 1), h)),
            _resident((s, LANES), lambda h, i: (0, pairs + h)),
            _resident((s, LANES), lambda h, i: (0, 2 * pairs + h)),
        ],
        out_specs=pl.BlockSpec((DIL_NB * DIL_TQ, LANES), lambda h, i: (i, h)),
        out_shape=jax.ShapeDtypeStruct((s, A_HEADS * HEAD_DIM), BF16),
        scratch_shapes=[
            pltpu.VMEM((2, 2 * DIL_DMAX + 1, DIL_TQ, DIL_BT), F32),
            pltpu.VMEM((2, 2, DIL_TQ, DIL_W), F32),
            pltpu.VMEM((2, 2, DIL_TQ, DIL_W), BF16),
        ],
        compiler_params=pltpu.CompilerParams(
            dimension_semantics=("arbitrary", "arbitrary"), vmem_limit_bytes=VMEM_LIMIT),
        name="dilated",
    )(t5_table, proj, proj, proj, proj)


NA_QROWS = 8
NA_KROWS = 16
NA_TQ = NA_QROWS * GRID_W
NA_TK = NA_KROWS * GRID_W
NA_RO = 2 * NA_ROWS - 1
NA_CO = 2 * NA_COLS - 1


NA_NB = 2
NA_ROWS_CHUNK = 32
NA_EDGE, NA_INNER = 0, 1


def _natten_kernel(rpb_ref, q_ref, qn_ref, k_ref, v_ref, o_ref,
                   tile_ref, bias_ref, s_ref, p_ref, *, rows):
    hp = pl.program_id(0)
    i = pl.program_id(1)
    n_steps = pl.num_programs(1)
    n_blocks = rows // NA_QROWS

    def key_row0(blk):
        return jnp.clip(blk * NA_QROWS - NA_ROWS // 2, 0, rows - NA_KROWS)

    def build_tiles():
        c = lax.broadcasted_iota(jnp.int32, (GRID_W, LANES), 0)
        kc = lax.broadcasted_iota(jnp.int32, (GRID_W, LANES), 1) & (GRID_W - 1)
        co = jnp.clip(kc - c + (NA_COLS - 1), 0, NA_CO - 1)
        c0 = jnp.clip(c - NA_COLS // 2, 0, GRID_W - NA_COLS)
        col_ok = (kc >= c0) & (kc < c0 + NA_COLS)
        for hh in range(2):
            head = 2 * hp + hh

            def body(ro, carry):
                base = (head * NA_RO + ro) * NA_CO
                v = jnp.full(co.shape, rpb_ref[base + NA_CO - 1], F32)
                for k in range(NA_CO - 2, -1, -1):
                    v = jnp.where(co == k, rpb_ref[base + k], v)
                tile_ref[hh, ro] = jnp.where(col_ok, v * LOG2E, NEG)
                return carry

            lax.fori_loop(0, NA_RO, body, 0)

    def build_bias(blk, which):
        rb = blk * NA_QROWS
        kb0 = key_row0(blk)
        left = lax.broadcasted_iota(jnp.int32, (GRID_W, LANES), 1) < GRID_W
        for qr in range(NA_QROWS):
            r = rb + qr
            r0 = jnp.clip(r - NA_ROWS // 2, 0, rows - NA_ROWS)
            for a in range(NA_KROWS // 2):
                kr = kb0 + 2 * a
                ro = kr - r + (NA_ROWS - 1)
                ok0 = (kr >= r0) & (kr < r0 + NA_ROWS)
                ok1 = (kr + 1 >= r0) & (kr + 1 < r0 + NA_ROWS)
                ro0 = jnp.clip(ro, 0, NA_RO - 1)
                ro1 = jnp.clip(ro + 1, 0, NA_RO - 1)
                for hh in range(2):
                    t0 = jnp.where(ok0, tile_ref[hh, ro0], NEG)
                    t1 = jnp.where(ok1, tile_ref[hh, ro1], NEG)
                    bias_ref[which, hh, qr * GRID_W:(qr + 1) * GRID_W,
                             a * LANES:(a + 1) * LANES] = jnp.where(left, t0, t1)

    @pl.when(i == 0)
    def _():
        build_tiles()
        build_bias(0, NA_EDGE)
        build_bias(1, NA_INNER)

    @pl.when(i == n_steps - 1)
    def _():
        build_bias(n_blocks - 1, NA_EDGE)

    lane = lax.broadcasted_iota(jnp.int32, (NA_TQ, LANES), 1)
    vlane = lax.broadcasted_iota(jnp.int16, (NA_TK, LANES), 1)

    def key_start(blk):
        return pl.multiple_of(key_row0(blk) * GRID_W, GRID_W * 4)

    def scores(blk, slot, q, after=None):
        if after is not None:
            q = jnp.where(i < 0, jnp.broadcast_to(after, q.shape).astype(q.dtype), q)
        kw = k_ref[pl.ds(key_start(blk), NA_TK), :]
        for hh in range(2):
            in_head = (lane >= hh * HEAD_DIM) & (lane < (hh + 1) * HEAD_DIM)
            s_ref[slot, hh] = _nt_dot(jnp.where(in_head, q, jnp.zeros_like(q)), kw)

    def softmax(blk, slot, after=None):
        which = jnp.where((blk == 0) | (blk == n_blocks - 1), NA_EDGE, NA_INNER)
        maxima = []
        for hh in range(2):
            for r0 in range(0, NA_TQ, NA_ROWS_CHUNK):
                rws = slice(r0, r0 + NA_ROWS_CHUNK)
                s = s_ref[slot, hh, rws, :] + bias_ref[which, hh, rws, :]
                m = jnp.max(s, axis=-1, keepdims=True)
                if after is not None:
                    m = jnp.maximum(m, jnp.where(i < 0, after[rws, hh:hh + 1].astype(F32), NEG))
                p_ref[slot, hh, rws, :] = jnp.exp2(s - m).astype(BF16)
                maxima.append(m)
        return jnp.concatenate(maxima[-(NA_TQ // NA_ROWS_CHUNK):], axis=0)

    def pv(blk, slot):
        vw = v_ref[pl.ds(key_start(blk), NA_TK), :]
        one = jnp.ones_like(vw)
        zero = jnp.zeros_like(vw)
        v0 = jnp.where(vlane < HEAD_DIM, vw, jnp.where(vlane == HEAD_DIM, one, zero))
        v1 = jnp.where(vlane >= HEAD_DIM, vw, jnp.where(vlane == 0, one, zero))
        o0 = jnp.dot(p_ref[slot, 0], v0, preferred_element_type=F32)
        o1 = jnp.dot(p_ref[slot, 1], v1, preferred_element_type=F32)
        out = jnp.where(lane < HEAD_DIM, o0 / o0[:, HEAD_DIM:HEAD_DIM + 1], o1 / o1[:, 0:1])
        return out.astype(o_ref.dtype)

    first = NA_NB * i
    nxt = NA_NB * jnp.minimum(i + 1, n_steps - 1)

    def q_rows(j):
        src, jj = (q_ref, j) if j < NA_NB else (qn_ref, j - NA_NB)
        return src[jj * NA_TQ:(jj + 1) * NA_TQ]

    def block(j):
        return first + j if j < NA_NB else nxt + (j - NA_NB)

    @pl.when(i == 0)
    def _():
        scores(block(0), 0, q_rows(0))
        scores(block(1), 1, q_rows(1))
        softmax(block(0), 0)

    out = None
    m_prev = None
    for j in range(NA_NB):
        scores(block(j + 2), j % 2, q_rows(j + 2), after=m_prev)
        m_prev = softmax(block(j + 1), (j + 1) % 2, after=out)
        out = pv(block(j), j % 2)
        o_ref[j * NA_TQ:(j + 1) * NA_TQ] = out


def _natten(proj, rpb_flat, col0):
    s = proj.shape[0]
    rows = s // GRID_W
    pairs = B_HEADS // 2
    assert s % (NA_NB * NA_TQ) == 0 and NA_NB % 2 == 0 and rows // NA_QROWS >= 3
    n_steps = s // (NA_NB * NA_TQ)
    return pl.pallas_call(
        functools.partial(_natten_kernel, rows=rows),
        grid=(pairs, n_steps),
        in_specs=[
            pl.BlockSpec(memory_space=pltpu.SMEM),
            pl.BlockSpec((NA_NB * NA_TQ, LANES), lambda h, i: (i, col0 + h)),
            pl.BlockSpec((NA_NB * NA_TQ, LANES),
                         lambda h, i: (jnp.minimum(i + 1, n_steps - 1), col0 + h)),
            _resident((s, LANES), lambda h, i: (0, col0 + pairs + h)),
            _resident((s, LANES), lambda h, i: (0, col0 + 2 * pairs + h)),
        ],
        out_specs=pl.BlockSpec((NA_NB * NA_TQ, LANES), lambda h, i: (i, h)),
        out_shape=jax.ShapeDtypeStruct((s, B_HEADS * HEAD_DIM), BF16),
        scratch_shapes=[
            pltpu.VMEM((2, NA_RO, GRID_W, LANES), F32),
            pltpu.VMEM((2, 2, NA_TQ, NA_TK), F32),
            pltpu.VMEM((2, 2, NA_TQ, NA_TK), F32),
            pltpu.VMEM((2, 2, NA_TQ, NA_TK), BF16),
        ],
        compiler_params=pltpu.CompilerParams(
            dimension_semantics=("arbitrary", "arbitrary"), vmem_limit_bytes=VMEM_LIMIT),
        name="natten",
    )(rpb_flat, proj, proj, proj, proj)


FF_CHUNK = 256


def _mix_ffn_kernel(*refs, n_o, final):
    x_ref = refs[0]
    o_refs = refs[1:1 + n_o]
    wo_refs = refs[1 + n_o:1 + 2 * n_o]
    g_ref, wg_ref, wu_ref, wd_ref = refs[1 + 2 * n_o:5 + 2 * n_o]
    rest = refs[5 + 2 * n_o:]
    gf_ref = rest[0] if final else None
    out_ref = rest[-1]

    y = x_ref[...]
    for o_ref, wo_ref in zip(o_refs, wo_refs):
        y = y + jnp.dot(o_ref[...], wo_ref[...], preferred_element_type=F32)
    h = _rms(y, g_ref[...]).astype(BF16)
    d_ff = wg_ref.shape[1]
    acc = jnp.zeros(y.shape, F32)
    for c in range(0, d_ff, FF_CHUNK):
        gate = jnp.dot(h, wg_ref[:, c:c + FF_CHUNK], preferred_element_type=F32)
        up = jnp.dot(h, wu_ref[:, c:c + FF_CHUNK], preferred_element_type=F32)
        act = (gate * jax.nn.sigmoid(gate) * up).astype(BF16)
        acc = acc + jnp.dot(act, wd_ref[c:c + FF_CHUNK, :], preferred_element_type=F32)
    y = y + acc
    if final:
        y = _rms(y, gf_ref[...])
    out_ref[...] = y


def _mix_ffn(x, o_parts, wo_parts, g, wg, wu, wd, gf, tm):
    s, d = x.shape
    d_ff = wg.shape[1]
    assert d_ff % FF_CHUNK == 0
    n_o = len(o_parts)
    final = gf is not None
    in_specs = [pl.BlockSpec((tm, d), lambda i: (i, 0))]
    in_specs += [pl.BlockSpec((tm, o.shape[1]), lambda i: (i, 0)) for o in o_parts]
    in_specs += [_resident(w.shape, lambda i: (0, 0)) for w in wo_parts]
    in_specs += [
        _resident((1, d), lambda i: (0, 0)),
        _resident((d, d_ff), lambda i: (0, 0)),
        _resident((d, d_ff), lambda i: (0, 0)),
        _resident((d_ff, d), lambda i: (0, 0)),
    ]
    args = [x, *o_parts, *wo_parts, g.reshape(1, d), wg, wu, wd]
    if final:
        in_specs.append(_resident((1, d), lambda i: (0, 0)))
        args.append(gf.reshape(1, d))
    return pl.pallas_call(
        functools.partial(_mix_ffn_kernel, n_o=n_o, final=final),
        grid=(s // tm,),
        in_specs=in_specs,
        out_specs=pl.BlockSpec((tm, d), lambda i: (i, 0)),
        out_shape=jax.ShapeDtypeStruct((s, d), F32),
        compiler_params=pltpu.CompilerParams(
            dimension_semantics=("parallel",), vmem_limit_bytes=VMEM_LIMIT),
        name="mix_ffn",
    )(*args)


ROPE_LANE0 = QK_NOPE
ROPE_HALF = QK_ROPE // 2


def _rope(x, cos_t, sin_lo, sin_hi):
    return (x * cos_t + pltpu.roll(x, LANES - ROPE_HALF, 1) * sin_lo
            + pltpu.roll(x, ROPE_HALF, 1) * sin_hi)


def _mla_prep_kernel(x_ref, g_ref, win_ref, qn_ref, kvn_ref, wuq_ref, wuk_ref, wuv_ref,
                     cos_ref, slo_ref, shi_ref, q_out, k_out, v_out):
    h = _rms(x_ref[...], g_ref[...]).astype(BF16)
    proj = jnp.dot(h, win_ref[...], preferred_element_type=F32)
    cq = _rms(proj[:, :Q_LORA], qn_ref[...]).astype(BF16)
    ckv = _rms(proj[:, Q_LORA:Q_LORA + KV_LORA], kvn_ref[...]).astype(BF16)
    cos_t, sin_lo, sin_hi = cos_ref[...], slo_ref[...], shi_ref[...]
    kr = _rope(proj[:, Q_LORA + KV_LORA:], cos_t, sin_lo, sin_hi)
    q = jnp.dot(cq, wuq_ref[...], preferred_element_type=F32)
    kn = jnp.dot(ckv, wuk_ref[...], preferred_element_type=F32)
    v = jnp.dot(ckv, wuv_ref[...], preferred_element_type=F32)
    lane = lax.broadcasted_iota(jnp.int32, (1, LANES), 1)
    one_col = jnp.where(lane == V_DIM, 1.0, 0.0)
    for hd in range(C_HEADS):
        sl = slice(hd * LANES, (hd + 1) * LANES)
        q_out[hd] = _rope(q[:, sl], cos_t, sin_lo, sin_hi).astype(BF16)
        k_out[hd] = (kn[:, sl] + kr).astype(BF16)
        v_out[hd] = (v[:, sl] + one_col).astype(BF16)


def _mla_prep(x, g, win, qn, kvn, wuq, wuk, wuv, cos_t, sin_lo, sin_hi, tm):
    s, d = x.shape
    const = lambda i: (0, 0)
    row = lambda i: (i, 0)
    head_out = jax.ShapeDtypeStruct((C_HEADS, s, LANES), BF16)
    head_spec = pl.BlockSpec((C_HEADS, tm, LANES), lambda i: (0, i, 0))
    return pl.pallas_call(
        _mla_prep_kernel,
        grid=(s // tm,),
        in_specs=[
            pl.BlockSpec((tm, d), row),
            _resident((1, d), const),
            _resident(win.shape, const),
            _resident((1, Q_LORA), const),
            _resident((1, KV_LORA), const),
            _resident(wuq.shape, const),
            _resident(wuk.shape, const),
            _resident(wuv.shape, const),
            pl.BlockSpec((tm, LANES), row),
            pl.BlockSpec((tm, LANES), row),
            pl.BlockSpec((tm, LANES), row),
        ],
        out_specs=[head_spec, head_spec, head_spec],
        out_shape=[head_out, head_out, head_out],
        compiler_params=pltpu.CompilerParams(
            dimension_semantics=("parallel",), vmem_limit_bytes=VMEM_LIMIT),
        name="mla_prep",
    )(x, g.reshape(1, d), win, qn.reshape(1, -1), kvn.reshape(1, -1), wuq, wuk, wuv,
      cos_t, sin_lo, sin_hi)


MLA_TQ = 512
MLA_TK = 512
MLA_SLAB = 2048
MLA_ROWS = 16
MLA_BAND = (T5_FAR - 1 + MLA_TK - 1) // MLA_TK
MLA_FAR = MLA_BAND + 1


def _mla_attn_kernel(t5_ref, q_ref, qn_ref, k_ref, v_ref, o_ref,
                     bias_ref, s_ref, p_ref, m_ref, acc_ref, *, seq):
    hp = pl.program_id(0)
    qi = pl.program_id(1)
    n_slab = seq // MLA_SLAB
    tiles = MLA_SLAB // MLA_TK
    half = T5_BUCKETS // 2
    far_neg = [t5_ref[half - 1, 2 * hp + hh] * LOG2E for hh in range(2)]
    far_pos = [t5_ref[2 * half - 1, 2 * hp + hh] * LOG2E for hh in range(2)]

    @pl.when(qi == 0)
    def _():
        row = lax.broadcasted_iota(jnp.int32, (MLA_TQ, MLA_TK), 0)
        col = lax.broadcasted_iota(jnp.int32, (MLA_TQ, MLA_TK), 1)
        for hh in range(2):
            bias_ref[hh, 0] = jnp.full((MLA_TQ, MLA_TK), far_neg[hh], F32)
            bias_ref[hh, 2 * MLA_FAR] = jnp.full((MLA_TQ, MLA_TK), far_pos[hh], F32)
            for d in range(-MLA_BAND, MLA_BAND + 1):
                bias_ref[hh, d + MLA_FAR] = _t5_bias_tile(
                    t5_ref, 2 * hp + hh, col - row + d * MLA_TK)

    m_ref[...] = jnp.full(m_ref.shape, NEG, F32)
    acc_ref[...] = jnp.zeros(acc_ref.shape, F32)

    def scores(slab, slot, q_src):
        kstart = pl.multiple_of(slab * MLA_SLAB, MLA_SLAB)
        for hh in range(2):
            s_ref[slot, hh] = _nt_dot(q_src[hh], k_ref[hh, pl.ds(kstart, MLA_SLAB), :])

    def softmax_pv(slab, slot, banded):
        kstart = pl.multiple_of(slab * MLA_SLAB, MLA_SLAB)
        for hh in range(2):
            if banded:
                idx = [jnp.clip(slab * tiles + u - qi, -MLA_FAR, MLA_FAR) + MLA_FAR
                       for u in range(tiles)]
            else:
                c = jnp.where(slab * tiles < qi, far_neg[hh], far_pos[hh])
            alphas = []
            for r0 in range(0, MLA_TQ, MLA_ROWS):
                rows = slice(r0, r0 + MLA_ROWS)
                s = s_ref[slot, hh, rows, :]
                if banded:
                    s = s + jnp.concatenate(
                        [bias_ref[hh, idx[u], rows, :] for u in range(tiles)], axis=1)
                    m_blk = jnp.max(s, axis=-1, keepdims=True)
                else:
                    m_blk = jnp.max(s, axis=-1, keepdims=True) + c
                m_old = m_ref[hh, rows, :]
                m_new = jnp.maximum(m_old, m_blk)
                m_ref[hh, rows, :] = m_new
                alphas.append(jnp.exp2(m_old - m_new))
                shift = m_new if banded else m_new - c
                p_ref[hh, rows, :] = jnp.exp2(s - shift).astype(BF16)
            alpha = jnp.concatenate(alphas, axis=0)
            acc_ref[hh] = alpha * acc_ref[hh] + jnp.dot(
                p_ref[hh], v_ref[hh, pl.ds(kstart, MLA_SLAB), :], preferred_element_type=F32)

    def work(slab, slot, nxt, banded):
        scores(nxt[0], 1 - slot, nxt[1])
        softmax_pv(slab, slot, banded)

    def is_banded(slab):
        d0 = slab * tiles - qi
        return (d0 >= -MLA_BAND - (tiles - 1)) & (d0 <= MLA_BAND)

    def stage(slab, slot, nxt):
        banded = is_banded(slab)
        pl.when(banded)(functools.partial(work, slab, slot, nxt, True))
        pl.when(jnp.logical_not(banded))(functools.partial(work, slab, slot, nxt, False))

    def stage_pair(slab, nxt):
        stage(slab, 0, (slab + 1, q_ref))
        stage(slab + 1, 1, nxt)

    pl.when(qi == 0)(functools.partial(scores, 0, 0, q_ref))

    def loop_body(t, carry):
        stage_pair(2 * t, (2 * t + 2, q_ref))
        return carry

    lax.fori_loop(0, n_slab // 2 - 1, loop_body, 0)
    stage_pair(n_slab - 2, (0, qn_ref))

    lane = lax.broadcasted_iota(jnp.int32, (MLA_TQ, LANES), 1)
    outs = [acc_ref[hh] / acc_ref[hh, :, V_DIM:V_DIM + 1] for hh in range(2)]
    out = jnp.where(lane < V_DIM, outs[0], pltpu.roll(outs[1], V_DIM, 1))
    o_ref[...] = out.astype(o_ref.dtype)


def _mla_attn(q, k, v, t5_table):
    heads, s, _ = q.shape
    pairs = heads // 2
    assert MLA_TQ == MLA_TK and s % (2 * MLA_SLAB) == 0 and MLA_SLAB % MLA_TK == 0
    nq = s // MLA_TQ
    return pl.pallas_call(
        functools.partial(_mla_attn_kernel, seq=s),
        grid=(pairs, nq),
        in_specs=[
            pl.BlockSpec(memory_space=pltpu.SMEM),
            pl.BlockSpec((2, MLA_TQ, LANES), lambda h, i: (h, i, 0)),
            pl.BlockSpec((2, MLA_TQ, LANES), lambda h, i: (h, jnp.minimum(i + 1, nq - 1), 0)),
            _resident((2, s, LANES), lambda h, i: (h, 0, 0)),
            _resident((2, s, LANES), lambda h, i: (h, 0, 0)),
        ],
        out_specs=pl.BlockSpec((MLA_TQ, LANES), lambda h, i: (i, h)),
        out_shape=jax.ShapeDtypeStruct((s, heads * V_DIM), BF16),
        scratch_shapes=[
            pltpu.VMEM((2, 2 * MLA_FAR + 1, MLA_TQ, MLA_TK), F32),
            pltpu.VMEM((2, 2, MLA_TQ, MLA_SLAB), F32),
            pltpu.VMEM((2, MLA_TQ, MLA_SLAB), BF16),
            pltpu.VMEM((2, MLA_TQ, 1), F32),
            pltpu.VMEM((2, MLA_TQ, LANES), F32),
        ],
        compiler_params=pltpu.CompilerParams(
            dimension_semantics=("arbitrary", "arbitrary"), vmem_limit_bytes=VMEM_LIMIT),
        name="mla_attn",
    )(t5_table, q, q, k, v)


def _per_head_lanes(w, heads, width, lane0=0):
    k = w.shape[0]
    w = w.reshape(k, heads, width)
    w = jnp.pad(w, ((0, 0), (0, 0), (lane0, LANES - lane0 - width)))
    return w.reshape(k, heads * LANES)


def _rope_tables(s):
    pos = jnp.arange(s, dtype=F32)
    freqs = ROPE_THETA ** (-jnp.arange(0, QK_ROPE, 2, dtype=F32) / QK_ROPE)
    ang = pos[:, None] * freqs[None, :]
    cos, sin = jnp.cos(ang), jnp.sin(ang)
    zeros = jnp.zeros((s, ROPE_HALF), F32)
    tail = LANES - ROPE_LANE0 - QK_ROPE
    cos_t = jnp.concatenate([jnp.ones((s, ROPE_LANE0), F32), cos, cos, jnp.ones((s, tail), F32)], axis=1)
    sin_lo = jnp.concatenate([jnp.zeros((s, ROPE_LANE0), F32), -sin, zeros, jnp.zeros((s, tail), F32)], axis=1)
    sin_hi = jnp.concatenate([jnp.zeros((s, ROPE_LANE0), F32), zeros, sin, jnp.zeros((s, tail), F32)], axis=1)
    return cos_t, sin_lo, sin_hi


def kernel(x, t5_table, attn_norm, ffn_norm, even_w_in, na_rpb, even_w_out, odd_w_in, mla_q_norm,
           mla_w_uq, mla_kv_norm, mla_w_uk, mla_w_uv, odd_w_out, ffn_w_gate, ffn_w_up, ffn_w_down,
           final_norm):
    b, s, d = x.shape
    assert b == 1 and s % (GRID_W * NA_KROWS) == 0 and s % MLA_TQ == 0
    x0 = x.reshape(s, d)
    tm = 512

    a_cols = A_HEADS * HEAD_DIM
    b_cols = B_HEADS * HEAD_DIM
    qscale = HEAD_DIM ** -0.5 * LOG2E
    w_in = even_w_in[0]
    col_scale = jnp.concatenate([
        jnp.full((a_cols,), qscale, F32), jnp.ones((2 * a_cols,), F32),
        jnp.full((b_cols,), qscale, F32), jnp.ones((2 * b_cols,), F32)])
    proj = _norm_proj(x0, attn_norm[0], (w_in * col_scale).astype(BF16), tm)
    oa = _dilated(proj, t5_table)
    ob = _natten(proj, na_rpb[0].reshape(-1), 3 * a_cols // LANES)
    w_out = even_w_out[0].astype(BF16)
    x1 = _mix_ffn(x0, [oa, ob], [w_out[:a_cols], w_out[a_cols:]], ffn_norm[0],
                  ffn_w_gate[0].astype(BF16), ffn_w_up[0].astype(BF16), ffn_w_down[0].astype(BF16),
                  None, tm)

    cscale = (QK_NOPE + QK_ROPE) ** -0.5 * LOG2E
    w_in1 = odd_w_in[0]
    win = jnp.concatenate([
        w_in1[:, :Q_LORA + KV_LORA],
        _per_head_lanes(w_in1[:, Q_LORA + KV_LORA:], 1, QK_ROPE, ROPE_LANE0)], axis=1).astype(BF16)
    wuq = (_per_head_lanes(mla_w_uq[0], C_HEADS, QK_NOPE + QK_ROPE) * cscale).astype(BF16)
    wuk = _per_head_lanes(mla_w_uk[0], C_HEADS, QK_NOPE).astype(BF16)
    wuv = _per_head_lanes(mla_w_uv[0], C_HEADS, V_DIM).astype(BF16)
    cos_t, sin_lo, sin_hi = _rope_tables(s)
    q1, k1, v1 = _mla_prep(x1, attn_norm[1], win, mla_q_norm[0], mla_kv_norm[0], wuq, wuk, wuv,
                           cos_t, sin_lo, sin_hi, tm)
    oc = _mla_attn(q1, k1, v1, t5_table)
    out = _mix_ffn(x1, [oc], [odd_w_out[0].astype(BF16)], ffn_norm[1],
                   ffn_w_gate[1].astype(BF16), ffn_w_up[1].astype(BF16), ffn_w_down[1].astype(BF16),
                   final_norm, tm)
    return out.reshape(b, s, d)
```

```python
import functools
import math

import numpy as np
import jax
import jax.numpy as jnp
from jax import lax
from jax.experimental import pallas as pl
from jax.experimental.pallas import tpu as pltpu

F32 = jnp.float32
BF16 = jnp.bfloat16

HEAD_DIM = 64
GRID_W = 64
RMS_EPS = 1e-6
A_HEADS = 8
B_HEADS = 8
DILATED_BRANCHES = ((128, 1), (512, 4), (2048, 16))
NA_ROWS = 8
NA_COLS = 16
C_HEADS = 16
Q_LORA = 384
KV_LORA = 128
QK_NOPE = 64
QK_ROPE = 32
V_DIM = 64
ROPE_THETA = 10000.0
T5_BUCKETS = 32
T5_MAX_DIST = 1024

LOG2E = math.log2(math.e)
NEG = -1e30
LANES = 128
VMEM_LIMIT = 62 * 1024 * 1024


def _t5_upper_bounds():
    half = T5_BUCKETS // 2
    max_exact = half // 2
    n = np.arange(0, 4 * T5_MAX_DIST, dtype=np.int64)
    nf = np.maximum(n, max_exact).astype(np.float32)
    val = (np.log(nf / np.float32(max_exact)) / np.float32(math.log(T5_MAX_DIST / max_exact))
           * np.float32(half - max_exact))
    large = np.minimum(max_exact + val.astype(np.int32), half - 1)
    bucket = np.where(n < max_exact, n, large)
    return tuple(int(np.argmax(bucket > b)) for b in range(half - 1))


T5_UPPER = _t5_upper_bounds()
T5_FAR = T5_UPPER[-1]


def _t5_bias_tile(t5_ref, head, rel):
    half = T5_BUCKETS // 2
    n = jnp.abs(rel)
    vneg = jnp.full(rel.shape, t5_ref[half - 1, head], F32)
    vpos = jnp.full(rel.shape, t5_ref[2 * half - 1, head], F32)
    for b in range(half - 2, -1, -1):
        inb = n < T5_UPPER[b]
        vneg = jnp.where(inb, t5_ref[b, head], vneg)
        vpos = jnp.where(inb, t5_ref[half + b, head], vpos)
    return jnp.where(rel > 0, vpos, vneg) * LOG2E


def _rms(x, g):
    return x * lax.rsqrt(jnp.mean(x * x, axis=-1, keepdims=True) + RMS_EPS) * g


def _nt_dot(a, b):
    return lax.dot_general(a, b, (((1,), (1,)), ((), ())), preferred_element_type=F32)


def _resident(shape, index_map):
    return pl.BlockSpec(shape, index_map, pipeline_mode=pl.Buffered(1))


def _norm_proj_kernel(x_ref, g_ref, w_ref, o_ref):
    h = _rms(x_ref[...], g_ref[...]).astype(BF16)
    o_ref[...] = jnp.dot(h, w_ref[...], preferred_element_type=F32).astype(o_ref.dtype)


def _norm_proj(x, g, w, tm):
    s, d = x.shape
    n = w.shape[1]
    return pl.pallas_call(
        _norm_proj_kernel,
        grid=(s // tm,),
        in_specs=[
            pl.BlockSpec((tm, d), lambda i: (i, 0)),
            _resident((1, d), lambda i: (0, 0)),
            _resident((d, n), lambda i: (0, 0)),
        ],
        out_specs=pl.BlockSpec((tm, n), lambda i: (i, 0)),
        out_shape=jax.ShapeDtypeStruct((s, n), BF16),
        compiler_params=pltpu.CompilerParams(
            dimension_semantics=("parallel",), vmem_limit_bytes=VMEM_LIMIT),
        name="norm_proj",
    )(x, g.reshape(1, d), w)


DIL_REACH = max(w // 2 for w, _ in DILATED_BRANCHES)
DIL_TQ = 256
DIL_NB = 4
DIL_ROWS = 16
DIL_W = DIL_TQ + 2 * DIL_REACH
DIL_BT = 256
DIL_WT = DIL_W // DIL_BT
DIL_DMAX = -(-(DIL_REACH + 1) // DIL_BT)


def _dilated_kernel(t5_ref, q_ref, qn_ref, k_ref, v_ref, o_ref, bias_ref, s_ref, p_ref, *, seq):
    hp = pl.program_id(0)
    i = pl.program_id(1)

    @pl.when(i == 0)
    def _():
        row = lax.broadcasted_iota(jnp.int32, (DIL_TQ, DIL_BT), 0)
        col = lax.broadcasted_iota(jnp.int32, (DIL_TQ, DIL_BT), 1)
        for d in range(-DIL_DMAX, DIL_DMAX + 1):
            rel = col - row + d * DIL_BT
            n = jnp.abs(rel)
            count = jnp.zeros(rel.shape, jnp.int32)
            for window, dil in DILATED_BRANCHES:
                member = ((rel & (dil - 1)) == 0) & (n <= (window // 2))
                count = count + member.astype(jnp.int32)
            logmult = jnp.where(count == 3, math.log2(3.0), jnp.where(count == 2, 1.0, 0.0))
            for hh in range(2):
                b = _t5_bias_tile(t5_ref, 2 * hp + hh, rel) + logmult
                bias_ref[hh, d + DIL_DMAX] = jnp.where(count > 0, b, NEG)

    lane = lax.broadcasted_iota(jnp.int32, (DIL_TQ, LANES), 1)
    vlane = lax.broadcasted_iota(jnp.int16, (DIL_W, LANES), 1)
    n_steps = pl.num_programs(1)

    def window(blk):
        t0 = blk * DIL_TQ
        start = pl.multiple_of(jnp.clip(t0 - DIL_REACH, 0, seq - DIL_W), DIL_BT)
        return start, (start - t0) // DIL_BT

    def scores(blk, slot, q, after=None):
        if after is not None:
            q = jnp.where(i < 0, jnp.broadcast_to(after, q.shape).astype(q.dtype), q)
        start, _ = window(blk)
        kw = k_ref[pl.ds(start, DIL_W), :]
        for hh in range(2):
            in_head = (lane >= hh * HEAD_DIM) & (lane < (hh + 1) * HEAD_DIM)
            s_ref[slot, hh] = _nt_dot(jnp.where(in_head, q, jnp.zeros_like(q)), kw)

    def softmax(blk, slot, after=None):
        _, d0 = window(blk)
        tile = [jnp.clip(d0 + k, -DIL_DMAX, DIL_DMAX) + DIL_DMAX for k in range(DIL_WT)]
        for hh in range(2):
            bias = jnp.concatenate([bias_ref[hh, tile[k]] for k in range(DIL_WT)], axis=1)
            s = s_ref[slot, hh] + bias
            m = jnp.max(s, axis=-1, keepdims=True)
            if after is not None:
                m = jnp.maximum(m, jnp.where(i < 0, after[:, hh:hh + 1].astype(F32), NEG))
            p_ref[slot, hh] = jnp.exp2(s - m).astype(BF16)
        return m

    def pv(blk, slot):
        start, _ = window(blk)
        vw = v_ref[pl.ds(start, DIL_W), :]
        one = jnp.ones_like(vw)
        zero = jnp.zeros_like(vw)
        v0 = jnp.where(vlane < HEAD_DIM, vw, jnp.where(vlane == HEAD_DIM, one, zero))
        v1 = jnp.where(vlane >= HEAD_DIM, vw, jnp.where(vlane == 0, one, zero))
        o0 = jnp.dot(p_ref[slot, 0], v0, preferred_element_type=F32)
        o1 = jnp.dot(p_ref[slot, 1], v1, preferred_element_type=F32)
        out = jnp.where(lane < HEAD_DIM, o0 / o0[:, HEAD_DIM:HEAD_DIM + 1], o1 / o1[:, 0:1])
        return out.astype(o_ref.dtype)

    first = DIL_NB * i
    nxt = DIL_NB * jnp.minimum(i + 1, n_steps - 1)

    def q_rows(j):
        src, jj = (q_ref, j) if j < DIL_NB else (qn_ref, j - DIL_NB)
        return src[jj * DIL_TQ:(jj + 1) * DIL_TQ]

    def block(j):
        return first + j if j < DIL_NB else nxt + (j - DIL_NB)

    @pl.when(i == 0)
    def _():
        scores(block(0), 0, q_rows(0))
        scores(block(1), 1, q_rows(1))
        softmax(block(0), 0)

    out = None
    m_prev = None
    for j in range(DIL_NB):
        scores(block(j + 2), j % 2, q_rows(j + 2), after=m_prev)
        m_prev = softmax(block(j + 1), (j + 1) % 2, after=out)
        out = pv(block(j), j % 2)
        o_ref[j * DIL_TQ:(j + 1) * DIL_TQ] = out


def _dilated(proj, t5_table):
    s = proj.shape[0]
    pairs = A_HEADS // 2
    assert s % (DIL_NB * DIL_TQ) == 0 and s >= DIL_W and DIL_NB % 2 == 0
    assert DIL_TQ % DIL_BT == 0 and DIL_REACH % DIL_BT == 0
    n_steps = s // (DIL_NB * DIL_TQ)
    return pl.pallas_call(
        functools.partial(_dilated_kernel, seq=s),
        grid=(pairs, n_steps),
        in_specs=[
            pl.BlockSpec(memory_space=pltpu.SMEM),
            pl.BlockSpec((DIL_NB * DIL_TQ, LANES), lambda h, i: (i, h)),
            pl.BlockSpec((DIL_NB * DIL_TQ, LANES),
                         lambda h, i: (jnp.minimum(i + 1, n_steps - 1), h)),
            _resident((s, LANES), lambda h, i: (0, pairs + h)),
            _resident((s, LANES), lambda h, i: (0, 2 * pairs + h)),
        ],
        out_specs=pl.BlockSpec((DIL_NB * DIL_TQ, LANES), lambda h, i: (i, h)),
        out_shape=jax.ShapeDtypeStruct((s, A_HEADS * HEAD_DIM), BF16),
        scratch_shapes=[
            pltpu.VMEM((2, 2 * DIL_DMAX + 1, DIL_TQ, DIL_BT), F32),
            pltpu.VMEM((2, 2, DIL_TQ, DIL_W), F32),
            pltpu.VMEM((2, 2, DIL_TQ, DIL_W), BF16),
        ],
        compiler_params=pltpu.CompilerParams(
            dimension_semantics=("arbitrary", "arbitrary"), vmem_limit_bytes=VMEM_LIMIT),
        name="dilated",
    )(t5_table, proj, proj, proj, proj)


NA_QROWS = 8
NA_KROWS = 16
NA_TQ = NA_QROWS * GRID_W
NA_TK = NA_KROWS * GRID_W
NA_RO = 2 * NA_ROWS - 1
NA_CO = 2 * NA_COLS - 1


NA_NB = 4
NA_ROWS_CHUNK = 32
NA_EDGE, NA_INNER = 0, 1


def _natten_kernel(rpb_ref, q_ref, qn_ref, k_ref, v_ref, o_ref,
                   tile_ref, bias_ref, s_ref, p_ref, *, rows):
    hp = pl.program_id(0)
    i = pl.program_id(1)
    n_steps = pl.num_programs(1)
    n_blocks = rows // NA_QROWS

    def key_row0(blk):
        return jnp.clip(blk * NA_QROWS - NA_ROWS // 2, 0, rows - NA_KROWS)

    def build_tiles():
        c = lax.broadcasted_iota(jnp.int32, (GRID_W, LANES), 0)
        kc = lax.broadcasted_iota(jnp.int32, (GRID_W, LANES), 1) & (GRID_W - 1)
        co = jnp.clip(kc - c + (NA_COLS - 1), 0, NA_CO - 1)
        c0 = jnp.clip(c - NA_COLS // 2, 0, GRID_W - NA_COLS)
        col_ok = (kc >= c0) & (kc < c0 + NA_COLS)
        for hh in range(2):
            head = 2 * hp + hh

            def body(ro, carry):
                base = (head * NA_RO + ro) * NA_CO
                v = jnp.full(co.shape, rpb_ref[base + NA_CO - 1], F32)
                for k in range(NA_CO - 2, -1, -1):
                    v = jnp.where(co == k, rpb_ref[base + k], v)
                tile_ref[hh, ro] = jnp.where(col_ok, v * LOG2E, NEG)
                return carry

            lax.fori_loop(0, NA_RO, body, 0)

    def build_bias(blk, which):
        rb = blk * NA_QROWS
        kb0 = key_row0(blk)
        left = lax.broadcasted_iota(jnp.int32, (GRID_W, LANES), 1) < GRID_W
        for qr in range(NA_QROWS):
            r = rb + qr
            r0 = jnp.clip(r - NA_ROWS // 2, 0, rows - NA_ROWS)
            for a in range(NA_KROWS // 2):
                kr = kb0 + 2 * a
                ro = kr - r + (NA_ROWS - 1)
                ok0 = (kr >= r0) & (kr < r0 + NA_ROWS)
                ok1 = (kr + 1 >= r0) & (kr + 1 < r0 + NA_ROWS)
                ro0 = jnp.clip(ro, 0, NA_RO - 1)
                ro1 = jnp.clip(ro + 1, 0, NA_RO - 1)
                for hh in range(2):
                    t0 = jnp.where(ok0, tile_ref[hh, ro0], NEG)
                    t1 = jnp.where(ok1, tile_ref[hh, ro1], NEG)
                    bias_ref[which, hh, qr * GRID_W:(qr + 1) * GRID_W,
                             a * LANES:(a + 1) * LANES] = jnp.where(left, t0, t1)

    @pl.when(i == 0)
    def _():
        build_tiles()
        build_bias(0, NA_EDGE)
        build_bias(1, NA_INNER)

    @pl.when(i == n_steps - 1)
    def _():
        build_bias(n_blocks - 1, NA_EDGE)

    lane = lax.broadcasted_iota(jnp.int32, (NA_TQ, LANES), 1)
    vlane = lax.broadcasted_iota(jnp.int16, (NA_TK, LANES), 1)

    def key_start(blk):
        return pl.multiple_of(key_row0(blk) * GRID_W, GRID_W * 4)

    def scores(blk, slot, q, after=None):
        if after is not None:
            q = jnp.where(i < 0, jnp.broadcast_to(after, q.shape).astype(q.dtype), q)
        kw = k_ref[pl.ds(key_start(blk), NA_TK), :]
        for hh in range(2):
            in_head = (lane >= hh * HEAD_DIM) & (lane < (hh + 1) * HEAD_DIM)
            s_ref[slot, hh] = _nt_dot(jnp.where(in_head, q, jnp.zeros_like(q)), kw)

    def softmax(blk, slot, after=None):
        which = jnp.where((blk == 0) | (blk == n_blocks - 1), NA_EDGE, NA_INNER)
        maxima = []
        for hh in range(2):
            for r0 in range(0, NA_TQ, NA_ROWS_CHUNK):
                rws = slice(r0, r0 + NA_ROWS_CHUNK)
                s = s_ref[slot, hh, rws, :] + bias_ref[which, hh, rws, :]
                m = jnp.max(s, axis=-1, keepdims=True)
                if after is not None:
                    m = jnp.maximum(m, jnp.where(i < 0, after[rws, hh:hh + 1].astype(F32), NEG))
                p_ref[slot, hh, rws, :] = jnp.exp2(s - m).astype(BF16)
                maxima.append(m)
        return jnp.concatenate(maxima[-(NA_TQ // NA_ROWS_CHUNK):], axis=0)

    def pv(blk, slot):
        vw = v_ref[pl.ds(key_start(blk), NA_TK), :]
        one = jnp.ones_like(vw)
        zero = jnp.zeros_like(vw)
        v0 = jnp.where(vlane < HEAD_DIM, vw, jnp.where(vlane == HEAD_DIM, one, zero))
        v1 = jnp.where(vlane >= HEAD_DIM, vw, jnp.where(vlane == 0, one, zero))
        o0 = jnp.dot(p_ref[slot, 0], v0, preferred_element_type=F32)
        o1 = jnp.dot(p_ref[slot, 1], v1, preferred_element_type=F32)
        out = jnp.where(lane < HEAD_DIM, o0 / o0[:, HEAD_DIM:HEAD_DIM + 1], o1 / o1[:, 0:1])
        return out.astype(o_ref.dtype)

    first = NA_NB * i
    nxt = NA_NB * jnp.minimum(i + 1, n_steps - 1)

    def q_rows(j):
        src, jj = (q_ref, j) if j < NA_NB else (qn_ref, j - NA_NB)
        return src[jj * NA_TQ:(jj + 1) * NA_TQ]

    def block(j):
        return first + j if j < NA_NB else nxt + (j - NA_NB)

    @pl.when(i == 0)
    def _():
        scores(block(0), 0, q_rows(0))
        scores(block(1), 1, q_rows(1))
        softmax(block(0), 0)

    out = None
    m_prev = None
    for j in range(NA_NB):
        scores(block(j + 2), j % 2, q_rows(j + 2), after=m_prev)
        m_prev = softmax(block(j + 1), (j + 1) % 2, after=out)
        out = pv(block(j), j % 2)
        o_ref[j * NA_TQ:(j + 1) * NA_TQ] = out


def _natten(proj, rpb_flat, col0):
    s = proj.shape[0]
    rows = s // GRID_W
    pairs = B_HEADS // 2
    assert s % (NA_NB * NA_TQ) == 0 and NA_NB % 2 == 0 and rows // NA_QROWS >= 3
    n_steps = s // (NA_NB * NA_TQ)
    return pl.pallas_call(
        functools.partial(_natten_kernel, rows=rows),
        grid=(pairs, n_steps),
        in_specs=[
            pl.BlockSpec(memory_space=pltpu.SMEM),
            pl.BlockSpec((NA_NB * NA_TQ, LANES), lambda h, i: (i, col0 + h)),
            pl.BlockSpec((NA_NB * NA_TQ, LANES),
                         lambda h, i: (jnp.minimum(i + 1, n_steps - 1), col0 + h)),
            _resident((s, LANES), lambda h, i: (0, col0 + pairs + h)),
            _resident((s, LANES), lambda h, i: (0, col0 + 2 * pairs + h)),
        ],
        out_specs=pl.BlockSpec((NA_NB * NA_TQ, LANES), lambda h, i: (i, h)),
        out_shape=jax.ShapeDtypeStruct((s, B_HEADS * HEAD_DIM), BF16),
        scratch_shapes=[
            pltpu.VMEM((2, NA_RO, GRID_W, LANES), F32),
            pltpu.VMEM((2, 2, NA_TQ, NA_TK), F32),
            pltpu.VMEM((2, 2, NA_TQ, NA_TK), F32),
            pltpu.VMEM((2, 2, NA_TQ, NA_TK), BF16),
        ],
        compiler_params=pltpu.CompilerParams(
            dimension_semantics=("arbitrary", "arbitrary"), vmem_limit_bytes=VMEM_LIMIT),
        name="natten",
    )(rpb_flat, proj, proj, proj, proj)


FF_CHUNK = 256


def _mix_ffn_kernel(*refs, n_o, final):
    x_ref = refs[0]
    o_refs = refs[1:1 + n_o]
    wo_refs = refs[1 + n_o:1 + 2 * n_o]
    g_ref, wg_ref, wu_ref, wd_ref = refs[1 + 2 * n_o:5 + 2 * n_o]
    rest = refs[5 + 2 * n_o:]
    gf_ref = rest[0] if final else None
    out_ref = rest[-1]

    y = x_ref[...]
    for o_ref, wo_ref in zip(o_refs, wo_refs):
        y = y + jnp.dot(o_ref[...], wo_ref[...], preferred_element_type=F32)
    h = _rms(y, g_ref[...]).astype(BF16)
    d_ff = wg_ref.shape[1]
    acc = jnp.zeros(y.shape, F32)
    for c in range(0, d_ff, FF_CHUNK):
        gate = jnp.dot(h, wg_ref[:, c:c + FF_CHUNK], preferred_element_type=F32)
        up = jnp.dot(h, wu_ref[:, c:c + FF_CHUNK], preferred_element_type=F32)
        act = (gate * jax.nn.sigmoid(gate) * up).astype(BF16)
        acc = acc + jnp.dot(act, wd_ref[c:c + FF_CHUNK, :], preferred_element_type=F32)
    y = y + acc
    if final:
        y = _rms(y, gf_ref[...])
    out_ref[...] = y


def _mix_ffn(x, o_parts, wo_parts, g, wg, wu, wd, gf, tm):
    s, d = x.shape
    d_ff = wg.shape[1]
    assert d_ff % FF_CHUNK == 0
    n_o = len(o_parts)
    final = gf is not None
    in_specs = [pl.BlockSpec((tm, d), lambda i: (i, 0))]
    in_specs += [pl.BlockSpec((tm, o.shape[1]), lambda i: (i, 0)) for o in o_parts]
    in_specs += [_resident(w.shape, lambda i: (0, 0)) for w in wo_parts]
    in_specs += [
        _resident((1, d), lambda i: (0, 0)),
        _resident((d, d_ff), lambda i: (0, 0)),
        _resident((d, d_ff), lambda i: (0, 0)),
        _resident((d_ff, d), lambda i: (0, 0)),
    ]
    args = [x, *o_parts, *wo_parts, g.reshape(1, d), wg, wu, wd]
    if final:
        in_specs.append(_resident((1, d), lambda i: (0, 0)))
        args.append(gf.reshape(1, d))
    return pl.pallas_call(
        functools.partial(_mix_ffn_kernel, n_o=n_o, final=final),
        grid=(s // tm,),
        in_specs=in_specs,
        out_specs=pl.BlockSpec((tm, d), lambda i: (i, 0)),
        out_shape=jax.ShapeDtypeStruct((s, d), F32),
        compiler_params=pltpu.CompilerParams(
            dimension_semantics=("parallel",), vmem_limit_bytes=VMEM_LIMIT),
        name="mix_ffn",
    )(*args)


ROPE_LANE0 = QK_NOPE
ROPE_HALF = QK_ROPE // 2


def _rope(x, cos_t, sin_lo, sin_hi):
    return (x * cos_t + pltpu.roll(x, LANES - ROPE_HALF, 1) * sin_lo
            + pltpu.roll(x, ROPE_HALF, 1) * sin_hi)


def _mla_prep_kernel(x_ref, g_ref, win_ref, qn_ref, kvn_ref, wuq_ref, wuk_ref, wuv_ref,
                     cos_ref, slo_ref, shi_ref, q_out, k_out, v_out):
    h = _rms(x_ref[...], g_ref[...]).astype(BF16)
    proj = jnp.dot(h, win_ref[...], preferred_element_type=F32)
    cq = _rms(proj[:, :Q_LORA], qn_ref[...]).astype(BF16)
    ckv = _rms(proj[:, Q_LORA:Q_LORA + KV_LORA], kvn_ref[...]).astype(BF16)
    cos_t, sin_lo, sin_hi = cos_ref[...], slo_ref[...], shi_ref[...]
    kr = _rope(proj[:, Q_LORA + KV_LORA:], cos_t, sin_lo, sin_hi)
    q = jnp.dot(cq, wuq_ref[...], preferred_element_type=F32)
    kn = jnp.dot(ckv, wuk_ref[...], preferred_element_type=F32)
    v = jnp.dot(ckv, wuv_ref[...], preferred_element_type=F32)
    lane = lax.broadcasted_iota(jnp.int32, (1, LANES), 1)
    one_col = jnp.where(lane == V_DIM, 1.0, 0.0)
    for hd in range(C_HEADS):
        sl = slice(hd * LANES, (hd + 1) * LANES)
        q_out[hd] = _rope(q[:, sl], cos_t, sin_lo, sin_hi).astype(BF16)
        k_out[hd] = (kn[:, sl] + kr).astype(BF16)
        v_out[hd] = (v[:, sl] + one_col).astype(BF16)


def _mla_prep(x, g, win, qn, kvn, wuq, wuk, wuv, cos_t, sin_lo, sin_hi, tm):
    s, d = x.shape
    const = lambda i: (0, 0)
    row = lambda i: (i, 0)
    head_out = jax.ShapeDtypeStruct((C_HEADS, s, LANES), BF16)
    head_spec = pl.BlockSpec((C_HEADS, tm, LANES), lambda i: (0, i, 0))
    return pl.pallas_call(
        _mla_prep_kernel,
        grid=(s // tm,),
        in_specs=[
            pl.BlockSpec((tm, d), row),
            _resident((1, d), const),
            _resident(win.shape, const),
            _resident((1, Q_LORA), const),
            _resident((1, KV_LORA), const),
            _resident(wuq.shape, const),
            _resident(wuk.shape, const),
            _resident(wuv.shape, const),
            pl.BlockSpec((tm, LANES), row),
            pl.BlockSpec((tm, LANES), row),
            pl.BlockSpec((tm, LANES), row),
        ],
        out_specs=[head_spec, head_spec, head_spec],
        out_shape=[head_out, head_out, head_out],
        compiler_params=pltpu.CompilerParams(
            dimension_semantics=("parallel",), vmem_limit_bytes=VMEM_LIMIT),
        name="mla_prep",
    )(x, g.reshape(1, d), win, qn.reshape(1, -1), kvn.reshape(1, -1), wuq, wuk, wuv,
      cos_t, sin_lo, sin_hi)


MLA_TQ = 512
MLA_TK = 512
MLA_SLAB = 2048
MLA_ROWS = 16
MLA_BAND = (T5_FAR - 1 + MLA_TK - 1) // MLA_TK
MLA_FAR = MLA_BAND + 1


def _mla_attn_kernel(t5_ref, q_ref, qn_ref, k_ref, v_ref, o_ref,
                     bias_ref, s_ref, p_ref, m_ref, acc_ref, *, seq):
    hp = pl.program_id(0)
    qi = pl.program_id(1)
    n_slab = seq // MLA_SLAB
    tiles = MLA_SLAB // MLA_TK
    half = T5_BUCKETS // 2
    far_neg = [t5_ref[half - 1, 2 * hp + hh] * LOG2E for hh in range(2)]
    far_pos = [t5_ref[2 * half - 1, 2 * hp + hh] * LOG2E for hh in range(2)]

    @pl.when(qi == 0)
    def _():
        row = lax.broadcasted_iota(jnp.int32, (MLA_TQ, MLA_TK), 0)
        col = lax.broadcasted_iota(jnp.int32, (MLA_TQ, MLA_TK), 1)
        for hh in range(2):
            bias_ref[hh, 0] = jnp.full((MLA_TQ, MLA_TK), far_neg[hh], F32)
            bias_ref[hh, 2 * MLA_FAR] = jnp.full((MLA_TQ, MLA_TK), far_pos[hh], F32)
            for d in range(-MLA_BAND, MLA_BAND + 1):
                bias_ref[hh, d + MLA_FAR] = _t5_bias_tile(
                    t5_ref, 2 * hp + hh, col - row + d * MLA_TK)

    m_ref[...] = jnp.full(m_ref.shape, NEG, F32)
    acc_ref[...] = jnp.zeros(acc_ref.shape, F32)

    def scores(slab, slot, q_src):
        kstart = pl.multiple_of(slab * MLA_SLAB, MLA_SLAB)
        for hh in range(2):
            s_ref[slot, hh] = _nt_dot(q_src[hh], k_ref[hh, pl.ds(kstart, MLA_SLAB), :])

    def softmax_pv(slab, slot, banded):
        kstart = pl.multiple_of(slab * MLA_SLAB, MLA_SLAB)
        for hh in range(2):
            if banded:
                idx = [jnp.clip(slab * tiles + u - qi, -MLA_FAR, MLA_FAR) + MLA_FAR
                       for u in range(tiles)]
            else:
                c = jnp.where(slab * tiles < qi, far_neg[hh], far_pos[hh])
            alphas = []
            for r0 in range(0, MLA_TQ, MLA_ROWS):
                rows = slice(r0, r0 + MLA_ROWS)
                s = s_ref[slot, hh, rows, :]
                if banded:
                    s = s + jnp.concatenate(
                        [bias_ref[hh, idx[u], rows, :] for u in range(tiles)], axis=1)
                    m_blk = jnp.max(s, axis=-1, keepdims=True)
                else:
                    m_blk = jnp.max(s, axis=-1, keepdims=True) + c
                m_old = m_ref[hh, rows, :]
                m_new = jnp.maximum(m_old, m_blk)
                m_ref[hh, rows, :] = m_new
                alphas.append(jnp.exp2(m_old - m_new))
                shift = m_new if banded else m_new - c
                p_ref[hh, rows, :] = jnp.exp2(s - shift).astype(BF16)
            alpha = jnp.concatenate(alphas, axis=0)
            acc_ref[hh] = alpha * acc_ref[hh] + jnp.dot(
                p_ref[hh], v_ref[hh, pl.ds(kstart, MLA_SLAB), :], preferred_element_type=F32)

    def work(slab, slot, nxt, banded):
        scores(nxt[0], 1 - slot, nxt[1])
        softmax_pv(slab, slot, banded)

    def is_banded(slab):
        d0 = slab * tiles - qi
        return (d0 >= -MLA_BAND - (tiles - 1)) & (d0 <= MLA_BAND)

    def stage(slab, slot, nxt):
        banded = is_banded(slab)
        pl.when(banded)(functools.partial(work, slab, slot, nxt, True))
        pl.when(jnp.logical_not(banded))(functools.partial(work, slab, slot, nxt, False))

    def stage_pair(slab, nxt):
        stage(slab, 0, (slab + 1, q_ref))
        stage(slab + 1, 1, nxt)

    pl.when(qi == 0)(functools.partial(scores, 0, 0, q_ref))

    def loop_body(t, carry):
        stage_pair(2 * t, (2 * t + 2, q_ref))
        return carry

    lax.fori_loop(0, n_slab // 2 - 1, loop_body, 0)
    stage_pair(n_slab - 2, (0, qn_ref))

    lane = lax.broadcasted_iota(jnp.int32, (MLA_TQ, LANES), 1)
    outs = [acc_ref[hh] / acc_ref[hh, :, V_DIM:V_DIM + 1] for hh in range(2)]
    out = jnp.where(lane < V_DIM, outs[0], pltpu.roll(outs[1], V_DIM, 1))
    o_ref[...] = out.astype(o_ref.dtype)


def _mla_attn(q, k, v, t5_table):
    heads, s, _ = q.shape
    pairs = heads // 2
    assert MLA_TQ == MLA_TK and s % (2 * MLA_SLAB) == 0 and MLA_SLAB % MLA_TK == 0
    nq = s // MLA_TQ
    return pl.pallas_call(
        functools.partial(_mla_attn_kernel, seq=s),
        grid=(pairs, nq),
        in_specs=[
            pl.BlockSpec(memory_space=pltpu.SMEM),
            pl.BlockSpec((2, MLA_TQ, LANES), lambda h, i: (h, i, 0)),
            pl.BlockSpec((2, MLA_TQ, LANES), lambda h, i: (h, jnp.minimum(i + 1, nq - 1), 0)),
            _resident((2, s, LANES), lambda h, i: (h, 0, 0)),
            _resident((2, s, LANES), lambda h, i: (h, 0, 0)),
        ],
        out_specs=pl.BlockSpec((MLA_TQ, LANES), lambda h, i: (i, h)),
        out_shape=jax.ShapeDtypeStruct((s, heads * V_DIM), BF16),
        scratch_shapes=[
            pltpu.VMEM((2, 2 * MLA_FAR + 1, MLA_TQ, MLA_TK), F32),
            pltpu.VMEM((2, 2, MLA_TQ, MLA_SLAB), F32),
            pltpu.VMEM((2, MLA_TQ, MLA_SLAB), BF16),
            pltpu.VMEM((2, MLA_TQ, 1), F32),
            pltpu.VMEM((2, MLA_TQ, LANES), F32),
        ],
        compiler_params=pltpu.CompilerParams(
            dimension_semantics=("arbitrary", "arbitrary"), vmem_limit_bytes=VMEM_LIMIT),
        name="mla_attn",
    )(t5_table, q, q, k, v)


def _per_head_lanes(w, heads, width, lane0=0):
    k = w.shape[0]
    w = w.reshape(k, heads, width)
    w = jnp.pad(w, ((0, 0), (0, 0), (lane0, LANES - lane0 - width)))
    return w.reshape(k, heads * LANES)


def _rope_tables(s):
    pos = jnp.arange(s, dtype=F32)
    freqs = ROPE_THETA ** (-jnp.arange(0, QK_ROPE, 2, dtype=F32) / QK_ROPE)
    ang = pos[:, None] * freqs[None, :]
    cos, sin = jnp.cos(ang), jnp.sin(ang)
    zeros = jnp.zeros((s, ROPE_HALF), F32)
    tail = LANES - ROPE_LANE0 - QK_ROPE
    cos_t = jnp.concatenate([jnp.ones((s, ROPE_LANE0), F32), cos, cos, jnp.ones((s, tail), F32)], axis=1)
    sin_lo = jnp.concatenate([jnp.zeros((s, ROPE_LANE0), F32), -sin, zeros, jnp.zeros((s, tail), F32)], axis=1)
    sin_hi = jnp.concatenate([jnp.zeros((s, ROPE_LANE0), F32), zeros, sin, jnp.zeros((s, tail), F32)], axis=1)
    return cos_t, sin_lo, sin_hi


def kernel(x, t5_table, attn_norm, ffn_norm, even_w_in, na_rpb, even_w_out, odd_w_in, mla_q_norm,
           mla_w_uq, mla_kv_norm, mla_w_uk, mla_w_uv, odd_w_out, ffn_w_gate, ffn_w_up, ffn_w_down,
           final_norm):
    b, s, d = x.shape
    assert b == 1 and s % (GRID_W * NA_KROWS) == 0 and s % MLA_TQ == 0
    x0 = x.reshape(s, d)
    tm = 512

    a_cols = A_HEADS * HEAD_DIM
    b_cols = B_HEADS * HEAD_DIM
    qscale = HEAD_DIM ** -0.5 * LOG2E
    w_in = even_w_in[0]
    col_scale = jnp.concatenate([
        jnp.full((a_cols,), qscale, F32), jnp.ones((2 * a_cols,), F32),
        jnp.full((b_cols,), qscale, F32), jnp.ones((2 * b_cols,), F32)])
    proj = _norm_proj(x0, attn_norm[0], (w_in * col_scale).astype(BF16), tm)
    oa = _dilated(proj, t5_table)
    ob = _natten(proj, na_rpb[0].reshape(-1), 3 * a_cols // LANES)
    w_out = even_w_out[0].astype(BF16)
    x1 = _mix_ffn(x0, [oa, ob], [w_out[:a_cols], w_out[a_cols:]], ffn_norm[0],
                  ffn_w_gate[0].astype(BF16), ffn_w_up[0].astype(BF16), ffn_w_down[0].astype(BF16),
                  None, tm)

    cscale = (QK_NOPE + QK_ROPE) ** -0.5 * LOG2E
    w_in1 = odd_w_in[0]
    win = jnp.concatenate([
        w_in1[:, :Q_LORA + KV_LORA],
        _per_head_lanes(w_in1[:, Q_LORA + KV_LORA:], 1, QK_ROPE, ROPE_LANE0)], axis=1).astype(BF16)
    wuq = (_per_head_lanes(mla_w_uq[0], C_HEADS, QK_NOPE + QK_ROPE) * cscale).astype(BF16)
    wuk = _per_head_lanes(mla_w_uk[0], C_HEADS, QK_NOPE).astype(BF16)
    wuv = _per_head_lanes(mla_w_uv[0], C_HEADS, V_DIM).astype(BF16)
    cos_t, sin_lo, sin_hi = _rope_tables(s)
    q1, k1, v1 = _mla_prep(x1, attn_norm[1], win, mla_q_norm[0], mla_kv_norm[0], wuq, wuk, wuv,
                           cos_t, sin_lo, sin_hi, tm)
    oc = _mla_attn(q1, k1, v1, t5_table)
    out = _mix_ffn(x1, [oc], [odd_w_out[0].astype(BF16)], ffn_norm[1],
                   ffn_w_gate[1].astype(BF16), ffn_w_up[1].astype(BF16), ffn_w_down[1].astype(BF16),
                   final_norm, tm)
    return out.reshape(b, s, d)
```
